```python
import jax, jax.numpy as jnp
from jax import lax
import numpy as np

D_MODEL = 1024
BATCH = 8
SEQ = 4096
DEPTH = 2

MEM_LEN = 256
N_BRANCH = 4
BRANCH_W = D_MODEL // 2
CONV_W = BRANCH_W
CONV_K = 31
MOBA_HEADS = 4
MOBA_HD = BRANCH_W // MOBA_HEADS
MOBA_BLOCK = 256
MOBA_TOPK = 3
MOBA_QCHUNK = 32
GLA_HEADS = 4
GLA_DK = BRANCH_W // (2 * GLA_HEADS)
GLA_DV = BRANCH_W // GLA_HEADS
GLA_RANK = 16
GLA_NORMALIZER = 16.0
GLA_CHUNK = 16
MEM_HEADS = 4
MEM_HD = BRANCH_W // MEM_HEADS
EPS = 1e-6

IN_SPLITS = (
    CONV_W, CONV_W, CONV_W,
    BRANCH_W, BRANCH_W, BRANCH_W, BRANCH_W,
    GLA_HEADS * GLA_DK, GLA_HEADS * GLA_DK, GLA_HEADS * GLA_DV, GLA_RANK, BRANCH_W,
    BRANCH_W, BRANCH_W,
    N_BRANCH * D_MODEL,
)
IN_COLS = sum(IN_SPLITS)
IN_OFFSETS = tuple(int(o) for o in np.cumsum(IN_SPLITS)[:-1])

kernel_name = "hybrid_conv_moba_gla_mem_gated"


def rmsnorm(x, g):
    xf = x.astype(jnp.float32)
    y = xf * lax.rsqrt(jnp.mean(xf * xf, axis=-1, keepdims=True) + EPS)
    return (y * g.astype(jnp.float32)).astype(x.dtype)


def layernorm(x, g, b):
    xf = x.astype(jnp.float32)
    mu = jnp.mean(xf, axis=-1, keepdims=True)
    var = jnp.mean(jnp.square(xf - mu), axis=-1, keepdims=True)
    y = (xf - mu) * lax.rsqrt(var + EPS)
    return (y * g.astype(jnp.float32) + b.astype(jnp.float32)).astype(x.dtype)


def conformer_conv(val, glu_gate, conv_w, conv_b, ln_g, ln_b):
    u = val * jax.nn.sigmoid(glu_gate)
    y = lax.conv_general_dilated(
        u, conv_w[:, None, :], (1,), [(CONV_K - 1, 0)],
        dimension_numbers=("NWC", "WIO", "NWC"), feature_group_count=CONV_W)
    y = y + conv_b
    return jax.nn.silu(layernorm(y, ln_g, ln_b))


def moba_attention(q, k, v):
    B, S, _ = q.shape
    H, hd, BLK, QC = MOBA_HEADS, MOBA_HD, MOBA_BLOCK, MOBA_QCHUNK
    nb = -(-S // BLK)
    pad = nb * BLK - S
    n_sel = min(MOBA_TOPK, nb)
    q = q.reshape(B, S, H, hd).transpose(0, 2, 1, 3) * (hd ** -0.5)
    k = k.reshape(B, S, H, hd).transpose(0, 2, 1, 3)
    v = v.reshape(B, S, H, hd).transpose(0, 2, 1, 3)
    kb = jnp.pad(k, ((0, 0), (0, 0), (0, pad), (0, 0))).reshape(B, H, nb, BLK, hd)
    vb = jnp.pad(v, ((0, 0), (0, 0), (0, pad), (0, 0))).reshape(B, H, nb, BLK, hd)
    kmean = jnp.mean(kb.astype(jnp.float32), axis=3)
    b_idx = jnp.arange(B)[:, None, None, None]
    h_idx = jnp.arange(H)[None, :, None, None]
    blk_ids = jnp.arange(nb)

    def chunk(c):
        start = c * QC
        qc = lax.dynamic_slice_in_dim(q, start, QC, axis=2)
        own = start // BLK
        qpos = start + jnp.arange(QC)
        rs = jnp.einsum("bhqd,bhnd->bhqn", qc.astype(jnp.float32), kmean)
        rs = jnp.where(blk_ids < own, rs, -jnp.inf)
        _, sel = lax.top_k(rs, n_sel)
        valid = sel < own
        k_sel = kb[b_idx, h_idx, sel]
        v_sel = vb[b_idx, h_idx, sel]
        s_sel = jnp.einsum("bhqd,bhqnkd->bhqnk", qc, k_sel).astype(jnp.float32)
        s_sel = jnp.where(valid[..., None], s_sel, -jnp.inf).reshape(B, H, QC, n_sel * BLK)
        k_own = lax.dynamic_index_in_dim(kb, own, axis=2, keepdims=False)
        v_own = lax.dynamic_index_in_dim(vb, own, axis=2, keepdims=False)
        kpos = own * BLK + jnp.arange(BLK)
        s_own = jnp.einsum("bhqd,bhkd->bhqk", qc, k_own).astype(jnp.float32)
        s_own = jnp.where(kpos[None, :] <= qpos[:, None], s_own, -jnp.inf)
        p = jax.nn.softmax(jnp.concatenate([s_sel, s_own], axis=-1), axis=-1).astype(v.dtype)
        p_sel = p[..., :n_sel * BLK].reshape(B, H, QC, n_sel, BLK)
        p_own = p[..., n_sel * BLK:]
        return (jnp.einsum("bhqnk,bhqnkd->bhqd", p_sel, v_sel)
                + jnp.einsum("bhqk,bhkd->bhqd", p_own, v_own))

    out = lax.map(chunk, jnp.arange(S // QC))
    return out.transpose(1, 0, 3, 2, 4).reshape(B, S, H * hd)


def gla_attention(q, k, v, gk_low, gk_w, gk_b, norm_g):
    B, S, _ = q.shape
    H, dk, dv, C = GLA_HEADS, GLA_DK, GLA_DV, GLA_CHUNK
    nc = S // C
    g = jax.nn.log_sigmoid((gk_low @ gk_w + gk_b).astype(jnp.float32)) / GLA_NORMALIZER

    def heads(t, d):
        return t.astype(jnp.float32).reshape(B, nc, C, H, d).transpose(0, 3, 1, 2, 4)

    qh = heads(q, dk) * (dk ** -0.5)
    kh = heads(k, dk)
    vh = heads(v, dv)
    G = jnp.cumsum(heads(g, dk), axis=3)
    idx = jnp.arange(C)
    causal = (idx[:, None] >= idx[None, :])[..., None]
    diff = G[:, :, :, :, None, :] - G[:, :, :, None, :, :]
    decay = jnp.exp(jnp.where(causal, diff, -jnp.inf))
    A = jnp.einsum("bhcid,bhcijd,bhcjd->bhcij", qh, decay, kh)
    o_intra = jnp.einsum("bhcij,bhcje->bhcie", A, vh)
    G_last = G[:, :, :, -1:, :]
    q_in = qh * jnp.exp(G)
    k_st = kh * jnp.exp(G_last - G)
    a_last = jnp.exp(G_last[:, :, :, 0, :])

    def step(state, xs):
        qi, ki, vi, ai = xs
        o = jnp.einsum("bhid,bhde->bhie", qi, state)
        state = ai[..., None] * state + jnp.einsum("bhjd,bhje->bhde", ki, vi)
        return state, o

    xs = (jnp.moveaxis(q_in, 2, 0), jnp.moveaxis(k_st, 2, 0),
          jnp.moveaxis(vh, 2, 0), jnp.moveaxis(a_last, 2, 0))
    _, o_inter = lax.scan(step, jnp.zeros((B, H, dk, dv), jnp.float32), xs)
    o = o_intra + jnp.moveaxis(o_inter, 0, 2)
    o = rmsnorm(o, norm_g)
    return o.transpose(0, 2, 3, 1, 4).reshape(B, S, H * dv).astype(q.dtype)


def memory_attention(q, mem_n, kv_w):
    B, S, _ = q.shape
    kv = mem_n @ kv_w
    k, v = jnp.split(kv, 2, axis=-1)
    qh = q.reshape(B, S, MEM_HEADS, MEM_HD) * (MEM_HD ** -0.5)
    kh = k.reshape(B, MEM_LEN, MEM_HEADS, MEM_HD)
    vh = v.reshape(B, MEM_LEN, MEM_HEADS, MEM_HD)
    s = jnp.einsum("bshd,bmhd->bhsm", qh, kh).astype(jnp.float32)
    p = jax.nn.softmax(s, axis=-1).astype(v.dtype)
    return jnp.einsum("bhsm,bmhd->bshd", p, vh).reshape(B, S, BRANCH_W)


def hybrid_layer(x, mem, norm_g, w_in, conv_w, conv_b, conv_ln_g, conv_ln_b,
                 gla_gk_w, gla_gk_b, gla_norm_g, mem_norm_g, mem_kv_w, w_branch, w_out):
    B, S, D = x.shape
    h = rmsnorm(x, norm_g)
    proj = h @ w_in
    (c_val, c_glu, c_z, m_q, m_k, m_v, m_z,
     g_q, g_k, g_v, g_low, g_z, x_q, x_z, gate_logits) = jnp.split(proj, IN_OFFSETS, axis=-1)
    y_conv = conformer_conv(c_val, c_glu, conv_w, conv_b, conv_ln_g, conv_ln_b) * jax.nn.silu(c_z)
    y_moba = moba_attention(m_q, m_k, m_v) * jax.nn.silu(m_z)
    y_gla = gla_attention(g_q, g_k, g_v, g_low, gla_gk_w, gla_gk_b, gla_norm_g) * jax.nn.silu(g_z)
    y_mem = memory_attention(x_q, rmsnorm(mem, mem_norm_g), mem_kv_w) * jax.nn.silu(x_z)
    ys = jnp.stack([y_conv, y_moba, y_gla, y_mem], axis=2)
    branch = jnp.einsum("bsnw,nwd->bsnd", ys, w_branch)
    gates = jax.nn.sigmoid(gate_logits.reshape(B, S, N_BRANCH, D))
    merged = jnp.sum(gates * branch, axis=2)
    return x + merged @ w_out


def setup_inputs(seed: int = 0) -> dict:
    key = jax.random.key(seed)
    ks = jax.random.split(key, 16)
    f32 = jnp.float32
    nrm = lambda k, shape, s: jax.random.normal(k, shape, f32) * s
    return {
        "x": nrm(ks[0], (BATCH, SEQ, D_MODEL), 1.0),
        "mem": nrm(ks[1], (BATCH, MEM_LEN, D_MODEL), 1.0),
        "norm_g": 1.0 + nrm(ks[2], (DEPTH, D_MODEL), 0.02),
        "w_in": nrm(ks[3], (DEPTH, D_MODEL, IN_COLS), D_MODEL ** -0.5),
        "conv_w": nrm(ks[4], (DEPTH, CONV_K, CONV_W), CONV_K ** -0.5),
        "conv_b": nrm(ks[5], (DEPTH, CONV_W), 0.02),
        "conv_ln_g": 1.0 + nrm(ks[6], (DEPTH, CONV_W), 0.02),
        "conv_ln_b": nrm(ks[7], (DEPTH, CONV_W), 0.02),
        "gla_gk_w": nrm(ks[8], (DEPTH, GLA_RANK, GLA_HEADS * GLA_DK), GLA_RANK ** -0.5),
        "gla_gk_b": nrm(ks[9], (DEPTH, GLA_HEADS * GLA_DK), 0.02),
        "gla_norm_g": 1.0 + nrm(ks[10], (DEPTH, GLA_DV), 0.02),
        "mem_norm_g": 1.0 + nrm(ks[11], (DEPTH, D_MODEL), 0.02),
        "mem_kv_w": nrm(ks[12], (DEPTH, D_MODEL, 2 * BRANCH_W), D_MODEL ** -0.5),
        "w_branch": nrm(ks[13], (DEPTH, N_BRANCH, BRANCH_W, D_MODEL), BRANCH_W ** -0.5),
        "w_out": nrm(ks[14], (DEPTH, D_MODEL, D_MODEL), D_MODEL ** -0.5),
        "final_norm_g": 1.0 + nrm(ks[15], (D_MODEL,), 0.02),
    }


def reference(x, mem, norm_g, w_in, conv_w, conv_b, conv_ln_g, conv_ln_b,
              gla_gk_w, gla_gk_b, gla_norm_g, mem_norm_g, mem_kv_w, w_branch, w_out,
              final_norm_g):
    for l in range(DEPTH):
        x = hybrid_layer(x, mem, norm_g[l], w_in[l], conv_w[l], conv_b[l], conv_ln_g[l],
                         conv_ln_b[l], gla_gk_w[l], gla_gk_b[l], gla_norm_g[l],
                         mem_norm_g[l], mem_kv_w[l], w_branch[l], w_out[l])
    return rmsnorm(x, final_norm_g)
```

```python
import functools

import jax
import jax.numpy as jnp
import numpy as np
from jax import lax
from jax.experimental import pallas as pl
from jax.experimental.pallas import tpu as pltpu

D_MODEL = 1024
DEPTH = 2
MEM_LEN = 256
N_BRANCH = 4
BRANCH_W = 512
CONV_K = 31
MOBA_HEADS = 4
MOBA_HD = 128
MOBA_BLOCK = 256
MOBA_TOPK = 3
GLA_HEADS = 4
GLA_DK = 64
GLA_DV = 128
GLA_RANK = 16
GLA_NORMALIZER = 16.0
MEM_HEADS = 4
MEM_HD = 128
EPS = 1e-6

LANES = 128
CONV_HALO = 32
GLA_CHUNK = 64
SEQ_TILE = 512
MERGE_TILE = 256
VMEM_LIMIT = 56 * 1024 * 1024

F32 = jnp.float32
BF16 = jnp.bfloat16
NEG_INF = float("-inf")

_SPLITS = (512, 512, 512, 512, 512, 512, 512, 256, 256, 512, GLA_RANK, 512, 512, 512, 4096)
_OFF = tuple(int(o) for o in np.cumsum((0,) + _SPLITS))
(_C_VAL, _C_GLU, _C_Z, _M_Q, _M_K, _M_V, _M_Z, _G_Q, _G_K, _G_V, _G_LOW, _G_Z, _X_Q, _X_Z,
 _GATES, _END) = _OFF


def _rms(xf, g):
    return xf * lax.rsqrt(jnp.mean(xf * xf, axis=-1, keepdims=True) + EPS) * g


def _sigmoid(x):
    return 1.0 / (1.0 + jnp.exp(-x))


def _silu(x):
    return x * _sigmoid(x)


def _dot(a, b):
    return jnp.dot(a, b, preferred_element_type=F32)


def _dot_nt(a, b):
    return lax.dot_general(a, b, (((1,), (1,)), ((), ())), preferred_element_type=F32)


def _dot_tn(a, b):
    return lax.dot_general(a, b, (((0,), (0,)), ((), ())), preferred_element_type=F32)


def _resident(shape):
    nd = len(shape)
    return pl.BlockSpec(shape, lambda *_: (0,) * nd, pipeline_mode=pl.Buffered(1))


def _mem_kv_body(mem_ref, g_ref, w_ref, k_ref, v_ref):
    m = _rms(mem_ref[...], g_ref[...]).astype(BF16)
    kv = _dot(m, w_ref[...])
    k_ref[...] = kv[:, :BRANCH_W].astype(BF16)
    v_ref[...] = kv[:, BRANCH_W:].astype(BF16)


def _mem_kv(mem, mem_norm_g, kv_w):
    B = mem.shape[0]
    out = jax.ShapeDtypeStruct((DEPTH, B, MEM_LEN, BRANCH_W), BF16)
    return pl.pallas_call(
        _mem_kv_body,
        grid=(DEPTH, B),
        in_specs=[
            pl.BlockSpec((None, MEM_LEN, D_MODEL), lambda l, b: (b, 0, 0)),
            pl.BlockSpec((None, 1, D_MODEL), lambda l, b: (l, 0, 0)),
            pl.BlockSpec((None, D_MODEL, 2 * BRANCH_W), lambda l, b: (l, 0, 0)),
        ],
        out_specs=[
            pl.BlockSpec((None, None, MEM_LEN, BRANCH_W), lambda l, b: (l, b, 0, 0)),
            pl.BlockSpec((None, None, MEM_LEN, BRANCH_W), lambda l, b: (l, b, 0, 0)),
        ],
        out_shape=[out, out],
        compiler_params=pltpu.CompilerParams(
            dimension_semantics=("arbitrary", "arbitrary"), vmem_limit_bytes=VMEM_LIMIT),
        name="mem_kv",
    )(mem, mem_norm_g.reshape(DEPTH, 1, D_MODEL), kv_w)


def _seq_proj_body(x_ref, ng_ref, w_ref, wlow_ref, gkw_ref, gkb_ref,
                   mq_ref, mk_ref, mv_ref, kmean_ref, gq_ref, gk_ref, gv_ref, gg_ref):
    T = x_ref.shape[0]
    h = _rms(x_ref[...], ng_ref[...]).astype(BF16)
    mq = _dot(h, w_ref[:, 0:512])
    mq_ref[...] = (mq * (MOBA_HD ** -0.5)).astype(BF16)
    mk = _dot(h, w_ref[:, 512:1024])
    mk_ref[...] = mk.astype(BF16)
    for j in range(T // MOBA_BLOCK):
        kmean_ref[j] = jnp.mean(mk[j * MOBA_BLOCK:(j + 1) * MOBA_BLOCK], axis=0, keepdims=True)
    mv_ref[...] = _dot(h, w_ref[:, 1024:1536]).astype(BF16)
    gq_ref[...] = _dot(h, w_ref[:, 1536:1792])
    gk_ref[...] = _dot(h, w_ref[:, 1792:2048])
    gv_ref[...] = _dot(h, w_ref[:, 2048:2560]).astype(BF16)
    low = _dot(h, wlow_ref[...]).astype(BF16)
    z = _dot(low, gkw_ref[...]) + gkb_ref[...]
    log_sig = jnp.minimum(z, 0.0) - jnp.log(1.0 + jnp.exp(-jnp.abs(z)))
    gg_ref[...] = log_sig / GLA_NORMALIZER


def _seq_proj(x, norm_g, w_seq, w_low, gk_w, gk_b):
    B, S, _ = x.shape
    T = SEQ_TILE
    nblk = T // MOBA_BLOCK
    tok = lambda w: pl.BlockSpec((None, T, w), lambda b, i: (b, i, 0))
    sd = jax.ShapeDtypeStruct
    return pl.pallas_call(
        _seq_proj_body,
        grid=(B, S // T),
        in_specs=[
            tok(D_MODEL),
            _resident((1, D_MODEL)),
            _resident(w_seq.shape),
            _resident(w_low.shape),
            _resident(gk_w.shape),
            _resident((1, GLA_HEADS * GLA_DK)),
        ],
        out_specs=[
            tok(512), tok(512), tok(512),
            pl.BlockSpec((None, nblk, 1, 512), lambda b, i: (b, i, 0, 0)),
            tok(256), tok(256), tok(512), tok(256),
        ],
        out_shape=[
            sd((B, S, 512), BF16), sd((B, S, 512), BF16), sd((B, S, 512), BF16),
            sd((B, S // MOBA_BLOCK, 1, 512), F32),
            sd((B, S, 256), F32), sd((B, S, 256), F32), sd((B, S, 512), BF16),
            sd((B, S, 256), F32),
        ],
        compiler_params=pltpu.CompilerParams(
            dimension_semantics=("arbitrary", "arbitrary"), vmem_limit_bytes=VMEM_LIMIT),
        name="seq_proj",
    )(x, norm_g.reshape(1, D_MODEL), w_seq, w_low, gk_w, gk_b.reshape(1, -1))


def _moba_body(q_ref, k_ref, v_ref, kmean_ref, o_ref):
    BLK = MOBA_BLOCK
    own = pl.program_id(2)
    nb = kmean_ref.shape[0]
    q = q_ref[...]

    rs = _dot_nt(q, kmean_ref[...].astype(BF16))
    col = lax.broadcasted_iota(jnp.int32, (BLK, nb), 1)
    rs = jnp.where(col < own, rs, NEG_INF)
    sel = jnp.zeros((BLK, nb), F32)
    for _ in range(min(MOBA_TOPK, nb)):
        mx = jnp.max(rs, axis=1, keepdims=True)
        first = jnp.min(jnp.where(rs == mx, col, nb), axis=1, keepdims=True)
        pick = (col == first) & (mx > NEG_INF)
        sel = jnp.where(pick, 1.0, sel)
        rs = jnp.where(col == first, NEG_INF, rs)

    start = pl.multiple_of(own * BLK, BLK)
    s = _dot_nt(q, k_ref[pl.ds(start, BLK), :])
    qi = lax.broadcasted_iota(jnp.int32, (BLK, BLK), 0)
    ki = lax.broadcasted_iota(jnp.int32, (BLK, BLK), 1)
    s = jnp.where(ki <= qi, s, NEG_INF)
    m0 = jnp.max(s, axis=1, keepdims=True)
    p = jnp.exp(s - m0)
    l0 = jnp.sum(p, axis=1, keepdims=True)
    acc0 = _dot(p.astype(BF16), v_ref[pl.ds(start, BLK), :])

    def past_block(n, carry):
        m, l, acc = carry
        off = pl.multiple_of(n * BLK, BLK)
        chosen = jnp.sum(jnp.where(col == n, sel, 0.0), axis=1, keepdims=True) > 0.0
        s = _dot_nt(q, k_ref[pl.ds(off, BLK), :])
        s = jnp.where(chosen, s, NEG_INF)
        m_new = jnp.maximum(m, jnp.max(s, axis=1, keepdims=True))
        alpha = jnp.exp(m - m_new)
        p = jnp.exp(s - m_new)
        l = alpha * l + jnp.sum(p, axis=1, keepdims=True)
        acc = alpha * acc + _dot(p.astype(BF16), v_ref[pl.ds(off, BLK), :])
        return m_new, l, acc

    _, l, acc = lax.fori_loop(0, own, past_block, (m0, l0, acc0))
    o_ref[...] = (acc / l).astype(o_ref.dtype)


def _moba(q, k, v, kmean):
    B, S, _ = q.shape
    nb = S // MOBA_BLOCK
    return pl.pallas_call(
        _moba_body,
        grid=(B, MOBA_HEADS, nb),
        in_specs=[
            pl.BlockSpec((None, MOBA_BLOCK, MOBA_HD), lambda b, h, i: (b, i, h)),
            pl.BlockSpec((None, S, MOBA_HD), lambda b, h, i: (b, 0, h)),
            pl.BlockSpec((None, S, MOBA_HD), lambda b, h, i: (b, 0, h)),
            pl.BlockSpec((None, nb, MOBA_HD), lambda b, h, i: (b, 0, h)),
        ],
        out_specs=pl.BlockSpec((None, MOBA_BLOCK, MOBA_HD), lambda b, h, i: (b, i, h)),
        out_shape=jax.ShapeDtypeStruct((B, S, BRANCH_W), BF16),
        compiler_params=pltpu.CompilerParams(
            dimension_semantics=("arbitrary", "arbitrary", "arbitrary"),
            vmem_limit_bytes=VMEM_LIMIT),
        name="moba",
    )(q, k, v, kmean)


def _gla_body(q_ref, k_ref, v_ref, g_ref, ng_ref, o_ref, state_ref):
    T = q_ref.shape[0]
    C = GLA_CHUNK
    H, dk, dv = GLA_HEADS, GLA_DK, GLA_DV

    @pl.when(pl.program_id(1) == 0)
    def _():
        state_ref[...] = jnp.zeros_like(state_ref)

    ri = lax.broadcasted_iota(jnp.int32, (C, C), 0)
    ci = lax.broadcasted_iota(jnp.int32, (C, C), 1)
    causal = ci <= ri
    tri = jnp.where(causal, 1.0, 0.0).astype(F32)
    ng = ng_ref[...]
    states = [state_ref[h] for h in range(H)]

    for c in range(T // C):
        rows = pl.ds(c * C, C)
        g = g_ref[rows, :]
        G = jnp.dot(tri, g, preferred_element_type=F32, precision=lax.Precision.HIGHEST)
        G_last = G[C - 1:C, :]
        q = q_ref[rows, :] * (dk ** -0.5)
        k = k_ref[rows, :]
        q_in = (q * jnp.exp(G)).astype(BF16)
        k_out = (k * jnp.exp(-G)).astype(BF16)
        k_st = (k * jnp.exp(G_last - G)).astype(BF16)
        a_last = jnp.exp(G_last)
        v = v_ref[rows, :]
        for h in range(H):
            ks = slice(h * dk, (h + 1) * dk)
            vs = slice(h * dv, (h + 1) * dv)
            A = jnp.where(causal, _dot_nt(q_in[:, ks], k_out[:, ks]), 0.0)
            o = _dot(A.astype(BF16), v[:, vs]) + _dot_nt(q_in[:, ks], states[h].astype(BF16))
            states[h] = a_last[:, ks] * states[h] + _dot_tn(v[:, vs], k_st[:, ks])
            o = o * lax.rsqrt(jnp.mean(o * o, axis=-1, keepdims=True) + EPS) * ng
            o_ref[rows, vs] = o.astype(o_ref.dtype)

    for h in range(H):
        state_ref[h] = states[h]


def _gla(q, k, v, g, norm_g):
    B, S, _ = q.shape
    T = SEQ_TILE
    tok = lambda w: pl.BlockSpec((None, T, w), lambda b, i: (b, i, 0))
    return pl.pallas_call(
        _gla_body,
        grid=(B, S // T),
        in_specs=[tok(256), tok(256), tok(512), tok(256), _resident((1, GLA_DV))],
        out_specs=tok(512),
        out_shape=jax.ShapeDtypeStruct((B, S, BRANCH_W), BF16),
        scratch_shapes=[pltpu.VMEM((GLA_HEADS, GLA_DV, GLA_DK), F32)],
        compiler_params=pltpu.CompilerParams(
            dimension_semantics=("arbitrary", "arbitrary"), vmem_limit_bytes=VMEM_LIMIT),
        name="gla",
    )(q, k, v, g, norm_g.reshape(1, GLA_DV))


def _merge_body(final, x_ref, ymoba_ref, ygla_ref, memk_ref, memv_ref, ng_ref, w_ref,
                convw_ref, convb_ref, lng_ref, lnb_ref, wb_ref, wo_ref, fg_ref,
                o_ref, u_ref):
    T = x_ref.shape[0]
    W = BRANCH_W
    x = x_ref[...]
    h = _rms(x, ng_ref[...]).astype(BF16)

    @pl.when(pl.program_id(1) == 0)
    def _():
        u_ref[0:CONV_HALO, :] = jnp.zeros((CONV_HALO, W), F32)

    u_ref[CONV_HALO:CONV_HALO + T, :] = (
        _dot(h, w_ref[:, 0:W]) * _sigmoid(_dot(h, w_ref[:, W:2 * W])))
    R = 32
    base = CONV_HALO - (CONV_K - 1)
    conv_rows = []
    for r in range(T // R):
        acc = jnp.zeros((R, W), F32) + convb_ref[...]
        for j in range(CONV_K):
            acc = acc + convw_ref[j:j + 1, :] * u_ref[base + j + r * R:base + j + (r + 1) * R, :]
        conv_rows.append(acc)
    yc = jnp.concatenate(conv_rows, axis=0)
    u_ref[0:CONV_HALO, :] = u_ref[T:T + CONV_HALO, :]
    mu = jnp.mean(yc, axis=-1, keepdims=True)
    yc = yc - mu
    var = jnp.mean(yc * yc, axis=-1, keepdims=True)
    yc = _silu(yc * lax.rsqrt(var + EPS) * lng_ref[...] + lnb_ref[...])
    y_conv = (yc * _silu(_dot(h, w_ref[:, 2 * W:3 * W]))).astype(BF16)

    y_moba = (ymoba_ref[...].astype(F32) * _silu(_dot(h, w_ref[:, 3 * W:4 * W]))).astype(BF16)
    y_gla = (ygla_ref[...].astype(F32) * _silu(_dot(h, w_ref[:, 4 * W:5 * W]))).astype(BF16)

    xq = (_dot(h, w_ref[:, 5 * W:6 * W]) * (MEM_HD ** -0.5)).astype(BF16)
    heads = []
    for hd in range(MEM_HEADS):
        ls = slice(hd * MEM_HD, (hd + 1) * MEM_HD)
        s = _dot_nt(xq[:, ls], memk_ref[:, ls])
        p = jnp.exp(s - jnp.max(s, axis=-1, keepdims=True))
        l = jnp.sum(p, axis=-1, keepdims=True)
        heads.append(_dot(p.astype(BF16), memv_ref[:, ls]) / l)
    y_mem = (jnp.concatenate(heads, axis=-1) * _silu(_dot(h, w_ref[:, 6 * W:7 * W]))).astype(BF16)

    merged = jnp.zeros((T, D_MODEL), F32)
    for n, y in enumerate((y_conv, y_moba, y_gla, y_mem)):
        g0 = 7 * W + n * D_MODEL
        gate = _sigmoid(_dot(h, w_ref[:, g0:g0 + D_MODEL]))
        merged = merged + gate * _dot(y, wb_ref[n])
    out = x + _dot(merged.astype(BF16), wo_ref[...])
    if final:
        out = _rms(out, fg_ref[...])
    o_ref[...] = out


def _merge(final, x, y_moba, y_gla, mem_k, mem_v, norm_g, w_loc, conv_w, conv_b, ln_g, ln_b,
           w_branch, w_out, final_g):
    B, S, _ = x.shape
    T = MERGE_TILE
    tok = lambda w: pl.BlockSpec((None, T, w), lambda b, i: (b, i, 0))
    mem = pl.BlockSpec((None, MEM_LEN, BRANCH_W), lambda b, i: (b, 0, 0))
    row = lambda a: a.reshape(1, -1)
    return pl.pallas_call(
        functools.partial(_merge_body, final),
        grid=(B, S // T),
        in_specs=[
            tok(D_MODEL), tok(BRANCH_W), tok(BRANCH_W), mem, mem,
            _resident((1, D_MODEL)),
            _resident(w_loc.shape),
            _resident(conv_w.shape),
            _resident((1, BRANCH_W)), _resident((1, BRANCH_W)), _resident((1, BRANCH_W)),
            _resident(w_branch.shape),
            _resident(w_out.shape),
            _resident((1, D_MODEL)),
        ],
        out_specs=tok(D_MODEL),
        out_shape=jax.ShapeDtypeStruct((B, S, D_MODEL), F32),
        scratch_shapes=[pltpu.VMEM((T + CONV_HALO, BRANCH_W), F32)],
        compiler_params=pltpu.CompilerParams(
            dimension_semantics=("arbitrary", "arbitrary"), vmem_limit_bytes=VMEM_LIMIT),
        name="merge_final" if final else "merge",
    )(x, y_moba, y_gla, mem_k, mem_v, row(norm_g), w_loc, conv_w, row(conv_b), row(ln_g),
      row(ln_b), w_branch, w_out, row(final_g))


def kernel(x, mem, norm_g, w_in, conv_w, conv_b, conv_ln_g, conv_ln_b, gla_gk_w, gla_gk_b,
           gla_norm_g, mem_norm_g, mem_kv_w, w_branch, w_out, final_norm_g):
    mem_k, mem_v = _mem_kv(mem, mem_norm_g, mem_kv_w.astype(BF16))
    for l in range(DEPTH):
        w = w_in[l]
        w_seq = jnp.concatenate([w[:, _M_Q:_M_Z], w[:, _G_Q:_G_LOW]], axis=1).astype(BF16)
        w_low = jnp.pad(w[:, _G_LOW:_G_Z], ((0, 0), (0, LANES - GLA_RANK))).astype(BF16)
        gk_w = jnp.pad(gla_gk_w[l], ((0, LANES - GLA_RANK), (0, 0))).astype(BF16)
        w_loc = jnp.concatenate(
            [w[:, _C_VAL:_M_Q], w[:, _M_Z:_G_Q], w[:, _G_Z:_END]], axis=1).astype(BF16)
        mq, mk, mv, kmean, gq, gk, gv, gg = _seq_proj(x, norm_g[l], w_seq, w_low, gk_w, gla_gk_b[l])
        y_moba = _moba(mq, mk, mv, kmean.reshape(x.shape[0], -1, BRANCH_W))
        y_gla = _gla(gq, gk, gv, gg, gla_norm_g[l])
        x = _merge(l == DEPTH - 1, x, y_moba, y_gla, mem_k[l], mem_v[l], norm_g[l], w_loc,
                   conv_w[l], conv_b[l], conv_ln_g[l], conv_ln_b[l],
                   w_branch[l].astype(BF16), w_out[l].astype(BF16), final_norm_g)
    return x
```

```python
import functools

import jax
import jax.numpy as jnp
import numpy as np
from jax import lax
from jax.experimental import pallas as pl
from jax.experimental.pallas import tpu as pltpu

D_MODEL = 1024
DEPTH = 2
MEM_LEN = 256
N_BRANCH = 4
BRANCH_W = 512
CONV_K = 31
MOBA_HEADS = 4
MOBA_HD = 128
MOBA_BLOCK = 256
MOBA_TOPK = 3
GLA_HEADS = 4
GLA_DK = 64
GLA_DV = 128
GLA_RANK = 16
GLA_NORMALIZER = 16.0
MEM_HEADS = 4
MEM_HD = 128
EPS = 1e-6

LANES = 128
SUBLANES = 8
CONV_HALO = 32
GLA_CHUNK = 64
SEQ_TILE = 512
MERGE_TILE = 256
VMEM_LIMIT = 56 * 1024 * 1024

F32 = jnp.float32
BF16 = jnp.bfloat16
NEG_INF = float("-inf")
LOG2_E = 1.4426950408889634

_SPLITS = (512, 512, 512, 512, 512, 512, 512, 256, 256, 512, GLA_RANK, 512, 512, 512, 4096)
_OFF = tuple(int(o) for o in np.cumsum((0,) + _SPLITS))
(_C_VAL, _C_GLU, _C_Z, _M_Q, _M_K, _M_V, _M_Z, _G_Q, _G_K, _G_V, _G_LOW, _G_Z, _X_Q, _X_Z,
 _GATES, _END) = _OFF


def _rms(xf, g):
    return xf * lax.rsqrt(jnp.mean(xf * xf, axis=-1, keepdims=True) + EPS) * g


def _sigmoid(x):
    return 1.0 / (1.0 + jnp.exp(-x))


def _silu(x):
    return x * _sigmoid(x)


def _dot(a, b):
    return jnp.dot(a, b, preferred_element_type=F32)


def _dot_nt(a, b):
    return lax.dot_general(a, b, (((1,), (1,)), ((), ())), preferred_element_type=F32)


def _dot_tn(a, b):
    return lax.dot_general(a, b, (((0,), (0,)), ((), ())), preferred_element_type=F32)


def _resident(shape):
    nd = len(shape)
    return pl.BlockSpec(shape, lambda *_: (0,) * nd, pipeline_mode=pl.Buffered(1))


def _mem_kv_body(mem_ref, g_ref, w_ref, k_ref, v_ref):
    m = _rms(mem_ref[...], g_ref[...]).astype(BF16)
    kv = _dot(m, w_ref[...])
    k_ref[...] = kv[:, :BRANCH_W].astype(BF16)
    v_ref[...] = kv[:, BRANCH_W:].astype(BF16)


def _mem_kv(mem, mem_norm_g, kv_w):
    B = mem.shape[0]
    out = jax.ShapeDtypeStruct((DEPTH, B, MEM_LEN, BRANCH_W), BF16)
    return pl.pallas_call(
        _mem_kv_body,
        grid=(DEPTH, B),
        in_specs=[
            pl.BlockSpec((None, MEM_LEN, D_MODEL), lambda l, b: (b, 0, 0)),
            pl.BlockSpec((None, 1, D_MODEL), lambda l, b: (l, 0, 0)),
            pl.BlockSpec((None, D_MODEL, 2 * BRANCH_W), lambda l, b: (l, 0, 0)),
        ],
        out_specs=[
            pl.BlockSpec((None, None, MEM_LEN, BRANCH_W), lambda l, b: (l, b, 0, 0)),
            pl.BlockSpec((None, None, MEM_LEN, BRANCH_W), lambda l, b: (l, b, 0, 0)),
        ],
        out_shape=[out, out],
        compiler_params=pltpu.CompilerParams(
            dimension_semantics=("arbitrary", "arbitrary"), vmem_limit_bytes=VMEM_LIMIT),
        name="mem_kv",
    )(mem, mem_norm_g.reshape(DEPTH, 1, D_MODEL), kv_w)


def _seq_proj_body(x_ref, ng_ref, w_ref, wlow_ref, gkw_ref, gkb_ref,
                   mq_ref, mk_ref, mv_ref, kmean_ref, gq_ref, gk_ref, gv_ref, gg_ref):
    T = x_ref.shape[0]
    h = _rms(x_ref[...], ng_ref[...]).astype(BF16)
    mq = _dot(h, w_ref[:, 0:512])
    mq_ref[...] = (mq * (MOBA_HD ** -0.5 * LOG2_E)).astype(BF16)
    mk = _dot(h, w_ref[:, 512:1024])
    mk_ref[...] = mk.astype(BF16)
    for j in range(T // MOBA_BLOCK):
        kmean_ref[j] = jnp.mean(mk[j * MOBA_BLOCK:(j + 1) * MOBA_BLOCK], axis=0, keepdims=True)
    mv_ref[...] = _dot(h, w_ref[:, 1024:1536]).T.astype(BF16)
    gq_ref[...] = _dot(h, w_ref[:, 1536:1792])
    gk_ref[...] = _dot(h, w_ref[:, 1792:2048])
    gv_ref[...] = _dot(h, w_ref[:, 2048:2560]).astype(BF16)
    low = _dot(h, wlow_ref[...]).astype(BF16)
    z = _dot(low, gkw_ref[...]) + gkb_ref[...]
    log_sig = jnp.minimum(z, 0.0) - jnp.log(1.0 + jnp.exp(-jnp.abs(z)))
    gg_ref[...] = log_sig / GLA_NORMALIZER


def _seq_proj(x, norm_g, w_seq, w_low, gk_w, gk_b):
    B, S, _ = x.shape
    T = SEQ_TILE
    nblk = T // MOBA_BLOCK
    tok = lambda w: pl.BlockSpec((None, T, w), lambda b, i: (b, i, 0))
    sd = jax.ShapeDtypeStruct
    return pl.pallas_call(
        _seq_proj_body,
        grid=(B, S // T),
        in_specs=[
            tok(D_MODEL),
            _resident((1, D_MODEL)),
            _resident(w_seq.shape),
            _resident(w_low.shape),
            _resident(gk_w.shape),
            _resident((1, GLA_HEADS * GLA_DK)),
        ],
        out_specs=[
            tok(512), tok(512),
            pl.BlockSpec((None, 512, T), lambda b, i: (b, 0, i)),
            pl.BlockSpec((None, nblk, 1, 512), lambda b, i: (b, i, 0, 0)),
            tok(256), tok(256), tok(512), tok(256),
        ],
        out_shape=[
            sd((B, S, 512), BF16), sd((B, S, 512), BF16), sd((B, 512, S), BF16),
            sd((B, S // MOBA_BLOCK, 1, 512), F32),
            sd((B, S, 256), F32), sd((B, S, 256), F32), sd((B, S, 512), BF16),
            sd((B, S, 256), F32),
        ],
        compiler_params=pltpu.CompilerParams(
            dimension_semantics=("arbitrary", "arbitrary"), vmem_limit_bytes=VMEM_LIMIT),
        name="seq_proj",
    )(x, norm_g.reshape(1, D_MODEL), w_seq, w_low, gk_w, gk_b.reshape(1, -1))


def _moba_body(q_ref, k_ref, vt_ref, kmean_ref, o_ref):
    BLK = MOBA_BLOCK
    S = q_ref.shape[0]
    nb = S // BLK

    rs = _dot_nt(kmean_ref[...].astype(BF16), q_ref[...])
    blk = lax.broadcasted_iota(jnp.int32, (nb, S), 0)
    own = lax.broadcasted_iota(jnp.int32, (nb, S), 1) // BLK
    rs = jnp.where(blk < own, rs, NEG_INF)
    bias = jnp.full((nb, S), NEG_INF, F32)
    for _ in range(min(MOBA_TOPK, nb)):
        mx = jnp.max(rs, axis=0, keepdims=True)
        first = jnp.min(jnp.where(rs == mx, blk, nb), axis=0, keepdims=True)
        pick = (blk == first) & (mx > NEG_INF)
        bias = jnp.where(pick, 0.0, bias)
        rs = jnp.where(blk == first, NEG_INF, rs)

    key_i = lax.broadcasted_iota(jnp.int32, (BLK, BLK), 0)
    qry_i = lax.broadcasted_iota(jnp.int32, (BLK, BLK), 1)
    causal = key_i <= qry_i

    def scores(i):
        return _dot_nt(k_ref[0:(i + 1) * BLK, :], q_ref[i * BLK:(i + 1) * BLK, :])

    st_next = scores(0)
    for i in range(nb):
        qs = slice(i * BLK, (i + 1) * BLK)
        nk = (i + 1) * BLK
        st = st_next
        if i + 1 < nb:
            st_next = scores(i + 1)
        parts =[st[n * BLK:(n + 1) * BLK] + bias[n:n + 1, qs] for n in range(i)]
        parts.append(jnp.where(causal, st[i * BLK:nk], NEG_INF))
        mx = parts[0]
        for part in parts[1:]:
            mx = jnp.maximum(mx, part)
        m = jnp.max(mx, axis=0, keepdims=True)
        ps = [jnp.exp2(part - m) for part in parts]
        tot = ps[0]
        for p in ps[1:]:
            tot = tot + p
        l = jnp.sum(tot, axis=0, keepdims=True)
        pt = jnp.concatenate([p.astype(BF16) for p in ps], axis=0)
        ot = _dot(vt_ref[:, 0:nk], pt) / l
        o_ref[qs, :] = ot.T.astype(o_ref.dtype)


def _moba(q, k, vt, kmean):
    B, S, _ = q.shape
    nb = S // MOBA_BLOCK
    seq = pl.BlockSpec((None, S, MOBA_HD), lambda b, h: (b, 0, h))
    return pl.pallas_call(
        _moba_body,
        grid=(B, MOBA_HEADS),
        in_specs=[
            seq, seq,
            pl.BlockSpec((None, MOBA_HD, S), lambda b, h: (b, h, 0)),
            pl.BlockSpec((None, nb, MOBA_HD), lambda b, h: (b, 0, h)),
        ],
        out_specs=seq,
        out_shape=jax.ShapeDtypeStruct((B, S, BRANCH_W), BF16),
        compiler_params=pltpu.CompilerParams(
            dimension_semantics=("arbitrary", "arbitrary"), vmem_limit_bytes=VMEM_LIMIT),
        name="moba",
    )(q, k, vt, kmean)


def _gla_body(q_ref, k_ref, v_ref, g_ref, ng_ref, o_ref, state_ref):
    T = q_ref.shape[0]
    C = GLA_CHUNK
    H, dk, dv = GLA_HEADS, GLA_DK, GLA_DV

    @pl.when(pl.program_id(1) == 0)
    def _():
        state_ref[...] = jnp.zeros_like(state_ref)

    ri = lax.broadcasted_iota(jnp.int32, (C, C), 0)
    ci = lax.broadcasted_iota(jnp.int32, (C, C), 1)
    causal = ci <= ri
    tri = jnp.where(causal, 1.0, 0.0).astype(F32)
    ng = ng_ref[...]
    states = [state_ref[h] for h in range(H)]

    for c in range(T // C):
        rows = pl.ds(c * C, C)
        g = g_ref[rows, :]
        G = jnp.dot(tri, g, preferred_element_type=F32, precision=lax.Precision.HIGHEST)
        G_last = G[C - 1:C, :]
        q = q_ref[rows, :] * (dk ** -0.5)
        k = k_ref[rows, :]
        q_in = (q * jnp.exp(G)).astype(BF16)
        k_out = (k * jnp.exp(-G)).astype(BF16)
        k_st = (k * jnp.exp(G_last - G)).astype(BF16)
        a_last = jnp.exp(G_last)
        v = v_ref[rows, :]
        for h in range(H):
            ks = slice(h * dk, (h + 1) * dk)
            vs = slice(h * dv, (h + 1) * dv)
            A = jnp.where(causal, _dot_nt(q_in[:, ks], k_out[:, ks]), 0.0)
            o = _dot(A.astype(BF16), v[:, vs]) + _dot_nt(q_in[:, ks], states[h].astype(BF16))
            states[h] = a_last[:, ks] * states[h] + _dot_tn(v[:, vs], k_st[:, ks])
            o = o * lax.rsqrt(jnp.mean(o * o, axis=-1, keepdims=True) + EPS) * ng
            o_ref[rows, vs] = o.astype(o_ref.dtype)

    for h in range(H):
        state_ref[h] = states[h]


def _gla(q, k, v, g, norm_g):
    B, S, _ = q.shape
    T = SEQ_TILE
    tok = lambda w: pl.BlockSpec((None, T, w), lambda b, i: (b, i, 0))
    return pl.pallas_call(
        _gla_body,
        grid=(B, S // T),
        in_specs=[tok(256), tok(256), tok(512), tok(256), _resident((1, GLA_DV))],
        out_specs=tok(512),
        out_shape=jax.ShapeDtypeStruct((B, S, BRANCH_W), BF16),
        scratch_shapes=[pltpu.VMEM((GLA_HEADS, GLA_DV, GLA_DK), F32)],
        compiler_params=pltpu.CompilerParams(
            dimension_semantics=("arbitrary", "arbitrary"), vmem_limit_bytes=VMEM_LIMIT),
        name="gla",
    )(q, k, v, g, norm_g.reshape(1, GLA_DV))


def _merge_body(final, x_ref, ymoba_ref, ygla_ref, memk_ref, memv_ref, ng_ref, w_ref,
                convw_ref, convb_ref, lng_ref, lnb_ref, wb_ref, wo_ref, fg_ref,
                o_ref, u_ref):
    T = x_ref.shape[0]
    W = BRANCH_W
    x = x_ref[...]
    h = _rms(x, ng_ref[...]).astype(BF16)

    @pl.when(pl.program_id(1) == 0)
    def _():
        u_ref[0, 0:CONV_HALO, :] = jnp.zeros((CONV_HALO, W), F32)

    u_ref[0, CONV_HALO:CONV_HALO + T, :] = (
        _dot(h, w_ref[:, 0:W]) * _sigmoid(_dot(h, w_ref[:, W:2 * W])))
    span = T + CONV_HALO - SUBLANES
    for s in range(1, SUBLANES):
        u_ref[s, 0:span, :] = u_ref[0, s:s + span, :]
    R = 32
    base = CONV_HALO - (CONV_K - 1)
    conv_rows = []
    gate_logits = []
    chunks = T // R
    for r in range(chunks):
        if r % (chunks // N_BRANCH) == 0:
            g0 = 7 * W + len(gate_logits) * D_MODEL
            gate_logits.append(_dot(h, w_ref[:, g0:g0 + D_MODEL]))
        acc = jnp.zeros((R, W), F32) + convb_ref[...]
        for j in range(CONV_K):
            a, s = divmod(base + j, SUBLANES)
            lo = a * SUBLANES + r * R
            acc = acc + convw_ref[j:j + 1, :] * u_ref[s, lo:lo + R, :]
        conv_rows.append(acc)
    yc = jnp.concatenate(conv_rows, axis=0)
    u_ref[0, 0:CONV_HALO, :] = u_ref[0, T:T + CONV_HALO, :]
    mu = jnp.mean(yc, axis=-1, keepdims=True)
    yc = yc - mu
    var = jnp.mean(yc * yc, axis=-1, keepdims=True)
    yc = _silu(yc * lax.rsqrt(var + EPS) * lng_ref[...] + lnb_ref[...])
    y_conv = (yc * _silu(_dot(h, w_ref[:, 2 * W:3 * W]))).astype(BF16)

    y_moba = (ymoba_ref[...].astype(F32) * _silu(_dot(h, w_ref[:, 3 * W:4 * W]))).astype(BF16)
    y_gla = (ygla_ref[...].astype(F32) * _silu(_dot(h, w_ref[:, 4 * W:5 * W]))).astype(BF16)

    xq = (_dot(h, w_ref[:, 5 * W:6 * W]) * (MEM_HD ** -0.5)).astype(BF16)
    heads = []
    for hd in range(MEM_HEADS):
        ls = slice(hd * MEM_HD, (hd + 1) * MEM_HD)
        s = _dot_nt(xq[:, ls], memk_ref[:, ls])
        p = jnp.exp(s - jnp.max(s, axis=-1, keepdims=True))
        l = jnp.sum(p, axis=-1, keepdims=True)
        heads.append(_dot(p.astype(BF16), memv_ref[:, ls]) / l)
    y_mem = (jnp.concatenate(heads, axis=-1) * _silu(_dot(h, w_ref[:, 6 * W:7 * W]))).astype(BF16)

    merged = jnp.zeros((T, D_MODEL), F32)
    for n, y in enumerate((y_conv, y_moba, y_gla, y_mem)):
        merged = merged + _sigmoid(gate_logits[n]) * _dot(y, wb_ref[n])
    out = x + _dot(merged.astype(BF16), wo_ref[...])
    if final:
        out = _rms(out, fg_ref[...])
    o_ref[...] = out


def _merge(final, x, y_moba, y_gla, mem_k, mem_v, norm_g, w_loc, conv_w, conv_b, ln_g, ln_b,
           w_branch, w_out, final_g):
    B, S, _ = x.shape
    T = MERGE_TILE
    tok = lambda w: pl.BlockSpec((None, T, w), lambda b, i: (b, i, 0))
    mem = pl.BlockSpec((None, MEM_LEN, BRANCH_W), lambda b, i: (b, 0, 0))
    row = lambda a: a.reshape(1, -1)
    return pl.pallas_call(
        functools.partial(_merge_body, final),
        grid=(B, S // T),
        in_specs=[
            tok(D_MODEL), tok(BRANCH_W), tok(BRANCH_W), mem, mem,
            _resident((1, D_MODEL)),
            _resident(w_loc.shape),
            _resident(conv_w.shape),
            _resident((1, BRANCH_W)), _resident((1, BRANCH_W)), _resident((1, BRANCH_W)),
            _resident(w_branch.shape),
            _resident(w_out.shape),
            _resident((1, D_MODEL)),
        ],
        out_specs=tok(D_MODEL),
        out_shape=jax.ShapeDtypeStruct((B, S, D_MODEL), F32),
        scratch_shapes=[pltpu.VMEM((SUBLANES, T + CONV_HALO, BRANCH_W), F32)],
        compiler_params=pltpu.CompilerParams(
            dimension_semantics=("arbitrary", "arbitrary"), vmem_limit_bytes=VMEM_LIMIT),
        name="merge_final" if final else "merge",
    )(x, y_moba, y_gla, mem_k, mem_v, row(norm_g), w_loc, conv_w, row(conv_b), row(ln_g),
      row(ln_b), w_branch, w_out, row(final_g))


def kernel(x, mem, norm_g, w_in, conv_w, conv_b, conv_ln_g, conv_ln_b, gla_gk_w, gla_gk_b,
           gla_norm_g, mem_norm_g, mem_kv_w, w_branch, w_out, final_norm_g):
    mem_k, mem_v = _mem_kv(mem, mem_norm_g, mem_kv_w.astype(BF16))
    for l in range(DEPTH):
        w = w_in[l]
        w_seq = jnp.concatenate([w[:, _M_Q:_M_Z], w[:, _G_Q:_G_LOW]], axis=1).astype(BF16)
        w_low = jnp.pad(w[:, _G_LOW:_G_Z], ((0, 0), (0, LANES - GLA_RANK))).astype(BF16)
        gk_w = jnp.pad(gla_gk_w[l], ((0, LANES - GLA_RANK), (0, 0))).astype(BF16)
        w_loc = jnp.concatenate(
            [w[:, _C_VAL:_M_Q], w[:, _M_Z:_G_Q], w[:, _G_Z:_END]], axis=1).astype(BF16)
        mq, mk, mv, kmean, gq, gk, gv, gg = _seq_proj(x, norm_g[l], w_seq, w_low, gk_w, gla_gk_b[l])
        y_moba = _moba(mq, mk, mv, kmean.reshape(x.shape[0], -1, BRANCH_W))
        y_gla = _gla(gq, gk, gv, gg, gla_norm_g[l])
        x = _merge(l == DEPTH - 1, x, y_moba, y_gla, mem_k[l], mem_v[l], norm_g[l], w_loc,
                   conv_w[l], conv_b[l], conv_ln_g[l], conv_ln_b[l],
                   w_branch[l].astype(BF16), w_out[l].astype(BF16), final_norm_g)
    return x
```

```python
import functools

import jax
import jax.numpy as jnp
import numpy as np
from jax import lax
from jax.experimental import pallas as pl
from jax.experimental.pallas import tpu as pltpu

D_MODEL = 1024
DEPTH = 2
MEM_LEN = 256
N_BRANCH = 4
BRANCH_W = 512
CONV_K = 31
MOBA_HEADS = 4
MOBA_HD = 128
MOBA_BLOCK = 256
MOBA_TOPK = 3
GLA_HEADS = 4
GLA_DK = 64
GLA_DV = 128
GLA_RANK = 16
GLA_NORMALIZER = 16.0
MEM_HEADS = 4
MEM_HD = 128
EPS = 1e-6

LANES = 128
SUBLANES = 8
CONV_HALO = 32
GLA_CHUNK = 128
GLA_SAFE_LOG_DECAY = 60.0
SEQ_TILE = 512
MERGE_TILE = 256
VMEM_LIMIT = 56 * 1024 * 1024

F32 = jnp.float32
BF16 = jnp.bfloat16
NEG_INF = float("-inf")
LOG2_E = 1.4426950408889634

_SPLITS = (512, 512, 512, 512, 512, 512, 512, 256, 256, 512, GLA_RANK, 512, 512, 512, 4096)
_OFF = tuple(int(o) for o in np.cumsum((0,) + _SPLITS))
(_C_VAL, _C_GLU, _C_Z, _M_Q, _M_K, _M_V, _M_Z, _G_Q, _G_K, _G_V, _G_LOW, _G_Z, _X_Q, _X_Z,
 _GATES, _END) = _OFF


def _rms(xf, g):
    return xf * lax.rsqrt(jnp.mean(xf * xf, axis=-1, keepdims=True) + EPS) * g


def _sigmoid(x):
    return 1.0 / (1.0 + jnp.exp(-x))


def _silu(x):
    return x * _sigmoid(x)


def _dot(a, b):
    return jnp.dot(a, b, preferred_element_type=F32)


def _dot_nt(a, b):
    return lax.dot_general(a, b, (((1,), (1,)), ((), ())), preferred_element_type=F32)


def _dot_tn(a, b):
    return lax.dot_general(a, b, (((0,), (0,)), ((), ())), preferred_element_type=F32)


def _resident(shape):
    nd = len(shape)
    return pl.BlockSpec(shape, lambda *_: (0,) * nd, pipeline_mode=pl.Buffered(1))


def _mem_kv_body(mem_ref, g_ref, w_ref, k_ref, v_ref):
    m = _rms(mem_ref[...], g_ref[...]).astype(BF16)
    kv = _dot(m, w_ref[...])
    k_ref[...] = kv[:, :BRANCH_W].astype(BF16)
    v_ref[...] = kv[:, BRANCH_W:].astype(BF16)


def _mem_kv(mem, mem_norm_g, kv_w):
    B = mem.shape[0]
    out = jax.ShapeDtypeStruct((DEPTH, B, MEM_LEN, BRANCH_W), BF16)
    return pl.pallas_call(
        _mem_kv_body,
        grid=(DEPTH, B),
        in_specs=[
            pl.BlockSpec((None, MEM_LEN, D_MODEL), lambda l, b: (b, 0, 0)),
            pl.BlockSpec((None, 1, D_MODEL), lambda l, b: (l, 0, 0)),
            pl.BlockSpec((None, D_MODEL, 2 * BRANCH_W), lambda l, b: (l, 0, 0)),
        ],
        out_specs=[
            pl.BlockSpec((None, None, MEM_LEN, BRANCH_W), lambda l, b: (l, b, 0, 0)),
            pl.BlockSpec((None, None, MEM_LEN, BRANCH_W), lambda l, b: (l, b, 0, 0)),
        ],
        out_shape=[out, out],
        compiler_params=pltpu.CompilerParams(
            dimension_semantics=("arbitrary", "arbitrary"), vmem_limit_bytes=VMEM_LIMIT),
        name="mem_kv",
    )(mem, mem_norm_g.reshape(DEPTH, 1, D_MODEL), kv_w)


def _seq_proj_body(x_ref, ng_ref, w_ref, wlow_ref, gkw_ref, gkb_ref,
                   mq_ref, mk_ref, mv_ref, kmean_ref, gq_ref, gk_ref, gv_ref, gcum_ref, glow_ref):
    T = x_ref.shape[0]
    h = _rms(x_ref[...], ng_ref[...]).astype(BF16)
    low = _dot(h, wlow_ref[...]).astype(BF16)
    z = _dot(low, gkw_ref[...]) + gkb_ref[...]
    mq = _dot(h, w_ref[:, 0:512])
    mq_ref[...] = (mq * (MOBA_HD ** -0.5 * LOG2_E)).astype(BF16)
    log_sig = jnp.minimum(z, 0.0) - jnp.log(1.0 + jnp.exp(-jnp.abs(z)))
    g = log_sig / GLA_NORMALIZER
    mk = _dot(h, w_ref[:, 512:1024])
    mk_ref[...] = mk.astype(BF16)
    for j in range(T // MOBA_BLOCK):
        kmean_ref[j] = jnp.mean(mk[j * MOBA_BLOCK:(j + 1) * MOBA_BLOCK], axis=0, keepdims=True)

    C = GLA_CHUNK
    ri = lax.broadcasted_iota(jnp.int32, (C, C), 0)
    ci = lax.broadcasted_iota(jnp.int32, (C, C), 1)
    tri = jnp.where(ci <= ri, 1.0, 0.0).astype(BF16)
    lowest = None
    for c in range(T // C):
        gc = g[c * C:(c + 1) * C]
        hi = gc.astype(BF16)
        rest = gc - hi.astype(F32)
        mid = rest.astype(BF16)
        lo = (rest - mid.astype(F32)).astype(BF16)
        G = _dot(tri, hi) + _dot(tri, mid) + _dot(tri, lo)
        gcum_ref[c * C:(c + 1) * C, :] = G
        lowest = G[C - 1:C, :] if lowest is None else jnp.minimum(lowest, G[C - 1:C, :])
    glow_ref[...] = lowest
    gq_ref[...] = _dot(h, w_ref[:, 1536:1792])
    gk_ref[...] = _dot(h, w_ref[:, 1792:2048])
    gv_ref[...] = _dot(h, w_ref[:, 2048:2560]).astype(BF16)
    mv_ref[...] = _dot(h, w_ref[:, 1024:1536]).T.astype(BF16)


def _seq_proj(x, norm_g, w_seq, w_low, gk_w, gk_b):
    B, S, _ = x.shape
    T = SEQ_TILE
    nblk = T // MOBA_BLOCK
    tok = lambda w: pl.BlockSpec((None, T, w), lambda b, i: (b, i, 0))
    sd = jax.ShapeDtypeStruct
    return pl.pallas_call(
        _seq_proj_body,
        grid=(B, S // T),
        in_specs=[
            tok(D_MODEL),
            _resident((1, D_MODEL)),
            _resident(w_seq.shape),
            _resident(w_low.shape),
            _resident(gk_w.shape),
            _resident((1, GLA_HEADS * GLA_DK)),
        ],
        out_specs=[
            tok(512), tok(512),
            pl.BlockSpec((None, 512, T), lambda b, i: (b, 0, i)),
            pl.BlockSpec((None, nblk, 1, 512), lambda b, i: (b, i, 0, 0)),
            tok(256), tok(256), tok(512), tok(256),
            pl.BlockSpec((None, None, 1, 256), lambda b, i: (b, i, 0, 0)),
        ],
        out_shape=[
            sd((B, S, 512), BF16), sd((B, S, 512), BF16), sd((B, 512, S), BF16),
            sd((B, S // MOBA_BLOCK, 1, 512), F32),
            sd((B, S, 256), F32), sd((B, S, 256), F32), sd((B, S, 512), BF16),
            sd((B, S, 256), F32), sd((B, S // T, 1, 256), F32),
        ],
        compiler_params=pltpu.CompilerParams(
            dimension_semantics=("arbitrary", "arbitrary"), vmem_limit_bytes=VMEM_LIMIT),
        name="seq_proj",
    )(x, norm_g.reshape(1, D_MODEL), w_seq, w_low, gk_w, gk_b.reshape(1, -1))


def _moba_body(q_ref, k_ref, vt_ref, kmean_ref, o_ref):
    BLK = MOBA_BLOCK
    S = q_ref.shape[0]
    nb = S // BLK

    rs = _dot_nt(kmean_ref[...].astype(BF16), q_ref[...])
    blk = lax.broadcasted_iota(jnp.int32, (nb, S), 0)
    own = lax.broadcasted_iota(jnp.int32, (nb, S), 1) // BLK
    rs = jnp.where(blk < own, rs, NEG_INF)
    bias = jnp.full((nb, S), NEG_INF, F32)
    for _ in range(min(MOBA_TOPK, nb)):
        mx = jnp.max(rs, axis=0, keepdims=True)
        first = jnp.min(jnp.where(rs == mx, blk, nb), axis=0, keepdims=True)
        pick = (blk == first) & (mx > NEG_INF)
        bias = jnp.where(pick, 0.0, bias)
        rs = jnp.where(blk == first, NEG_INF, rs)

    key_i = lax.broadcasted_iota(jnp.int32, (BLK, BLK), 0)
    qry_i = lax.broadcasted_iota(jnp.int32, (BLK, BLK), 1)
    causal = key_i <= qry_i

    def scores(i):
        return _dot_nt(k_ref[0:(i + 1) * BLK, :], q_ref[i * BLK:(i + 1) * BLK, :])

    st_next = scores(0)
    for i in range(nb):
        qs = slice(i * BLK, (i + 1) * BLK)
        nk = (i + 1) * BLK
        st = st_next
        if i + 1 < nb:
            st_next = scores(i + 1)
        parts =[st[n * BLK:(n + 1) * BLK] + bias[n:n + 1, qs] for n in range(i)]
        parts.append(jnp.where(causal, st[i * BLK:nk], NEG_INF))
        mx = parts[0]
        for part in parts[1:]:
            mx = jnp.maximum(mx, part)
        m = jnp.max(mx, axis=0, keepdims=True)
        ps = [jnp.exp2(part - m) for part in parts]
        tot = ps[0]
        for p in ps[1:]:
            tot = tot + p
        l = jnp.sum(tot, axis=0, keepdims=True)
        pt = jnp.concatenate([p.astype(BF16) for p in ps], axis=0)
        ot = _dot(vt_ref[:, 0:nk], pt) / l
        o_ref[qs, :] = ot.T.astype(o_ref.dtype)


def _moba(q, k, vt, kmean):
    B, S, _ = q.shape
    nb = S // MOBA_BLOCK
    seq = pl.BlockSpec((None, S, MOBA_HD), lambda b, h: (b, 0, h))
    return pl.pallas_call(
        _moba_body,
        grid=(B, MOBA_HEADS),
        in_specs=[
            seq, seq,
            pl.BlockSpec((None, MOBA_HD, S), lambda b, h: (b, h, 0)),
            pl.BlockSpec((None, nb, MOBA_HD), lambda b, h: (b, 0, h)),
        ],
        out_specs=seq,
        out_shape=jax.ShapeDtypeStruct((B, S, BRANCH_W), BF16),
        compiler_params=pltpu.CompilerParams(
            dimension_semantics=("arbitrary", "arbitrary"), vmem_limit_bytes=VMEM_LIMIT),
        name="moba",
    )(q, k, vt, kmean)


def _gla_body(q_ref, k_ref, v_ref, gcum_ref, glow_ref, ng_ref, o_ref, state_ref, vf_ref):
    T = q_ref.shape[0]
    C = GLA_CHUNK
    H, dk, dv = GLA_HEADS, GLA_DK, GLA_DV
    n_chunks = T // C

    @pl.when(pl.program_id(1) == 0)
    def _():
        state_ref[...] = jnp.zeros_like(state_ref)

    ri = lax.broadcasted_iota(jnp.int32, (C, C), 0)
    ci = lax.broadcasted_iota(jnp.int32, (C, C), 1)
    causal = ci <= ri
    factorable = jnp.min(glow_ref[...]) >= -GLA_SAFE_LOG_DECAY

    def run_tile(intra_fn):
        ng = ng_ref[...]
        states = [state_ref[h] for h in range(H)]
        for c in range(n_chunks):
            rows = pl.ds(c * C, C)
            G = gcum_ref[rows, :]
            G_last = G[C - 1:C, :]
            q = q_ref[rows, :] * (dk ** -0.5)
            k = k_ref[rows, :]
            q_in = (q * jnp.exp(G)).astype(BF16)
            k_st = (k * jnp.exp(G_last - G)).astype(BF16)
            a_last = jnp.exp(G_last)
            v = v_ref[rows, :]
            intra = intra_fn(c, q, k, G, q_in, v)
            for h in range(H):
                ks = slice(h * dk, (h + 1) * dk)
                vs = slice(h * dv, (h + 1) * dv)
                o = intra[h] + _dot_nt(q_in[:, ks], states[h].astype(BF16))
                states[h] = a_last[:, ks] * states[h] + _dot_tn(v[:, vs], k_st[:, ks])
                o = o * lax.rsqrt(jnp.mean(o * o, axis=-1, keepdims=True) + EPS) * ng
                o_ref[rows, vs] = o.astype(o_ref.dtype)
        for h in range(H):
            state_ref[h] = states[h]

    def intra_factored(c, q, k, G, q_in, v):
        k_out = (k * jnp.exp(-G)).astype(BF16)
        out = []
        for h in range(H):
            ks = slice(h * dk, (h + 1) * dk)
            A = jnp.where(causal, _dot_nt(q_in[:, ks], k_out[:, ks]), 0.0)
            out.append(_dot(A.astype(BF16), v[:, h * dv:(h + 1) * dv]))
        return out

    def intra_keywise(c, q, k, G, q_in, v):
        d_head = lax.broadcasted_iota(jnp.int32, (H * dk, H * dv), 0) // dk
        e_head = lax.broadcasted_iota(jnp.int32, (H * dk, H * dv), 1) // dv
        head_spread = jnp.where(d_head == e_head, 1.0, 0.0).astype(F32)
        row = lax.broadcasted_iota(jnp.int32, (C, H * dk), 0)

        def add_key(j, acc):
            r = c * C + j
            k_j = k_ref[pl.ds(r, 1), :]
            G_j = gcum_ref[pl.ds(r, 1), :]
            v_j = vf_ref[pl.ds(r, 1), :]
            w = jnp.where(row >= j, q * k_j * jnp.exp(jnp.minimum(G - G_j, 0.0)), 0.0)
            a = jnp.dot(w, head_spread, preferred_element_type=F32,
                        precision=lax.Precision.HIGHEST)
            return acc + a * v_j

        o = lax.fori_loop(0, C, add_key, jnp.zeros((C, H * dv), F32))
        return [o[:, h * dv:(h + 1) * dv] for h in range(H)]

    @pl.when(factorable)
    def _():
        run_tile(intra_factored)

    @pl.when(jnp.logical_not(factorable))
    def _():
        vf_ref[...] = v_ref[...].astype(F32)
        run_tile(intra_keywise)


def _gla(q, k, v, gcum, glow, norm_g):
    B, S, _ = q.shape
    T = SEQ_TILE
    tok = lambda w: pl.BlockSpec((None, T, w), lambda b, i: (b, i, 0))
    return pl.pallas_call(
        _gla_body,
        grid=(B, S // T),
        in_specs=[tok(256), tok(256), tok(512), tok(256),
                  pl.BlockSpec((None, None, 1, 256), lambda b, i: (b, i, 0, 0)),
                  _resident((1, GLA_DV))],
        out_specs=tok(512),
        out_shape=jax.ShapeDtypeStruct((B, S, BRANCH_W), BF16),
        scratch_shapes=[pltpu.VMEM((GLA_HEADS, GLA_DV, GLA_DK), F32),
                        pltpu.VMEM((T, GLA_HEADS * GLA_DV), F32)],
        compiler_params=pltpu.CompilerParams(
            dimension_semantics=("arbitrary", "arbitrary"), vmem_limit_bytes=VMEM_LIMIT),
        name="gla",
    )(q, k, v, gcum, glow, norm_g.reshape(1, GLA_DV))


def _merge_body(final, x_ref, ymoba_ref, ygla_ref, memk_ref, memv_ref, ng_ref, w_ref,
                convw_ref, convb_ref, lng_ref, lnb_ref, wb_ref, wo_ref, fg_ref,
                o_ref, u_ref):
    T = x_ref.shape[0]
    W = BRANCH_W
    x = x_ref[...]
    h = _rms(x, ng_ref[...]).astype(BF16)

    @pl.when(pl.program_id(1) == 0)
    def _():
        u_ref[0, 0:CONV_HALO, :] = jnp.zeros((CONV_HALO, W), F32)

    u_ref[0, CONV_HALO:CONV_HALO + T, :] = (
        _dot(h, w_ref[:, 0:W]) * _sigmoid(_dot(h, w_ref[:, W:2 * W])))
    span = T + CONV_HALO - SUBLANES
    for s in range(1, SUBLANES):
        u_ref[s, 0:span, :] = u_ref[0, s:s + span, :]
    R = 32
    base = CONV_HALO - (CONV_K - 1)
    conv_rows = []
    gate_logits = []
    chunks = T // R
    for r in range(chunks):
        if r % (chunks // N_BRANCH) == 0:
            g0 = 7 * W + len(gate_logits) * D_MODEL
            gate_logits.append(_dot(h, w_ref[:, g0:g0 + D_MODEL]))
        acc = jnp.zeros((R, W), F32) + convb_ref[...]
        for j in range(CONV_K):
            a, s = divmod(base + j, SUBLANES)
            lo = a * SUBLANES + r * R
            acc = acc + convw_ref[j:j + 1, :] * u_ref[s, lo:lo + R, :]
        conv_rows.append(acc)
    yc = jnp.concatenate(conv_rows, axis=0)
    u_ref[0, 0:CONV_HALO, :] = u_ref[0, T:T + CONV_HALO, :]
    mu = jnp.mean(yc, axis=-1, keepdims=True)
    yc = yc - mu
    var = jnp.mean(yc * yc, axis=-1, keepdims=True)
    yc = _silu(yc * lax.rsqrt(var + EPS) * lng_ref[...] + lnb_ref[...])
    y_conv = (yc * _silu(_dot(h, w_ref[:, 2 * W:3 * W]))).astype(BF16)

    y_moba = (ymoba_ref[...].astype(F32) * _silu(_dot(h, w_ref[:, 3 * W:4 * W]))).astype(BF16)
    y_gla = (ygla_ref[...].astype(F32) * _silu(_dot(h, w_ref[:, 4 * W:5 * W]))).astype(BF16)

    xq = (_dot(h, w_ref[:, 5 * W:6 * W]) * (MEM_HD ** -0.5)).astype(BF16)
    heads = []
    for hd in range(MEM_HEADS):
        ls = slice(hd * MEM_HD, (hd + 1) * MEM_HD)
        s = _dot_nt(xq[:, ls], memk_ref[:, ls])
        p = jnp.exp(s - jnp.max(s, axis=-1, keepdims=True))
        l = jnp.sum(p, axis=-1, keepdims=True)
        heads.append(_dot(p.astype(BF16), memv_ref[:, ls]) / l)
    y_mem = (jnp.concatenate(heads, axis=-1) * _silu(_dot(h, w_ref[:, 6 * W:7 * W]))).astype(BF16)

    merged = jnp.zeros((T, D_MODEL), F32)
    for n, y in enumerate((y_conv, y_moba, y_gla, y_mem)):
        merged = merged + _sigmoid(gate_logits[n]) * _dot(y, wb_ref[n])
    out = x + _dot(merged.astype(BF16), wo_ref[...])
    if final:
        out = _rms(out, fg_ref[...])
    o_ref[...] = out


def _merge(final, x, y_moba, y_gla, mem_k, mem_v, norm_g, w_loc, conv_w, conv_b, ln_g, ln_b,
           w_branch, w_out, final_g):
    B, S, _ = x.shape
    T = MERGE_TILE
    tok = lambda w: pl.BlockSpec((None, T, w), lambda b, i: (b, i, 0))
    mem = pl.BlockSpec((None, MEM_LEN, BRANCH_W), lambda b, i: (b, 0, 0))
    row = lambda a: a.reshape(1, -1)
    return pl.pallas_call(
        functools.partial(_merge_body, final),
        grid=(B, S // T),
        in_specs=[
            tok(D_MODEL), tok(BRANCH_W), tok(BRANCH_W), mem, mem,
            _resident((1, D_MODEL)),
            _resident(w_loc.shape),
            _resident(conv_w.shape),
            _resident((1, BRANCH_W)), _resident((1, BRANCH_W)), _resident((1, BRANCH_W)),
            _resident(w_branch.shape),
            _resident(w_out.shape),
            _resident((1, D_MODEL)),
        ],
        out_specs=tok(D_MODEL),
        out_shape=jax.ShapeDtypeStruct((B, S, D_MODEL), F32),
        scratch_shapes=[pltpu.VMEM((SUBLANES, T + CONV_HALO, BRANCH_W), F32)],
        compiler_params=pltpu.CompilerParams(
            dimension_semantics=("arbitrary", "arbitrary"), vmem_limit_bytes=VMEM_LIMIT),
        name="merge_final" if final else "merge",
    )(x, y_moba, y_gla, mem_k, mem_v, row(norm_g), w_loc, conv_w, row(conv_b), row(ln_g),
      row(ln_b), w_branch, w_out, row(final_g))


def kernel(x, mem, norm_g, w_in, conv_w, conv_b, conv_ln_g, conv_ln_b, gla_gk_w, gla_gk_b,
           gla_norm_g, mem_norm_g, mem_kv_w, w_branch, w_out, final_norm_g):
    mem_k, mem_v = _mem_kv(mem, mem_norm_g, mem_kv_w.astype(BF16))
    for l in range(DEPTH):
        w = w_in[l]
        w_seq = jnp.concatenate([w[:, _M_Q:_M_Z], w[:, _G_Q:_G_LOW]], axis=1).astype(BF16)
        w_low = jnp.pad(w[:, _G_LOW:_G_Z], ((0, 0), (0, LANES - GLA_RANK))).astype(BF16)
        gk_w = jnp.pad(gla_gk_w[l], ((0, LANES - GLA_RANK), (0, 0))).astype(BF16)
        w_loc = jnp.concatenate(
            [w[:, _C_VAL:_M_Q], w[:, _M_Z:_G_Q], w[:, _G_Z:_END]], axis=1).astype(BF16)
        mq, mk, mv, kmean, gq, gk, gv, gcum, glow = _seq_proj(
            x, norm_g[l], w_seq, w_low, gk_w, gla_gk_b[l])
        y_moba = _moba(mq, mk, mv, kmean.reshape(x.shape[0], -1, BRANCH_W))
        y_gla = _gla(gq, gk, gv, gcum, glow, gla_norm_g[l])
        x = _merge(l == DEPTH - 1, x, y_moba, y_gla, mem_k[l], mem_v[l], norm_g[l], w_loc,
                   conv_w[l], conv_b[l], conv_ln_g[l], conv_ln_b[l],
                   w_branch[l].astype(BF16), w_out[l].astype(BF16), final_norm_g)
    return x
```

```python
import functools

import jax
import jax.numpy as jnp
import numpy as np
from jax import lax
from jax.experimental import pallas as pl
from jax.experimental.pallas import tpu as pltpu

D_MODEL = 1024
DEPTH = 2
MEM_LEN = 256
N_BRANCH = 4
BRANCH_W = 512
CONV_K = 31
MOBA_HEADS = 4
MOBA_HD = 128
MOBA_BLOCK = 256
MOBA_TOPK = 3
GLA_HEADS = 4
GLA_DK = 64
GLA_DV = 128
GLA_RANK = 16
GLA_NORMALIZER = 16.0
MEM_HEADS = 4
MEM_HD = 128
EPS = 1e-6

LANES = 128
SUBLANES = 8
BF16_ROWS = 2 * SUBLANES
CONV_HALO = 32
GLA_CHUNK = 128
GLA_SAFE_LOG_DECAY = 60.0
SEQ_TILE = 512
MERGE_TILE = 256
VMEM_LIMIT = 56 * 1024 * 1024

F32 = jnp.float32
BF16 = jnp.bfloat16
NEG_INF = float("-inf")
LOG2_E = 1.4426950408889634

_SPLITS = (512, 512, 512, 512, 512, 512, 512, 256, 256, 512, GLA_RANK, 512, 512, 512, 4096)
_OFF = tuple(int(o) for o in np.cumsum((0,) + _SPLITS))
(_C_VAL, _C_GLU, _C_Z, _M_Q, _M_K, _M_V, _M_Z, _G_Q, _G_K, _G_V, _G_LOW, _G_Z, _X_Q, _X_Z,
 _GATES, _END) = _OFF


def _rms(xf, g):
    return xf * lax.rsqrt(jnp.mean(xf * xf, axis=-1, keepdims=True) + EPS) * g


def _sigmoid(x):
    return 1.0 / (1.0 + jnp.exp(-x))


def _silu(x):
    return x * _sigmoid(x)


def _dot(a, b):
    return jnp.dot(a, b, preferred_element_type=F32)


def _dot_nt(a, b):
    return lax.dot_general(a, b, (((1,), (1,)), ((), ())), preferred_element_type=F32)


def _dot_tn(a, b):
    return lax.dot_general(a, b, (((0,), (0,)), ((), ())), preferred_element_type=F32)


def _resident(shape):
    nd = len(shape)
    return pl.BlockSpec(shape, lambda *_: (0,) * nd, pipeline_mode=pl.Buffered(1))


def _layer_resident(layer, shape):
    nd = len(shape)
    return pl.BlockSpec((None,) + tuple(shape), lambda *_: (layer,) + (0,) * nd,
                        pipeline_mode=pl.Buffered(1))


def _mem_kv_body(mem_ref, g_ref, w_ref, k_ref, v_ref):
    m = _rms(mem_ref[...], g_ref[...]).astype(BF16)
    kv = _dot(m, w_ref[...])
    k_ref[...] = kv[:, :BRANCH_W].astype(BF16)
    v_ref[...] = kv[:, BRANCH_W:].astype(BF16)


def _mem_kv(mem, mem_norm_g, kv_w):
    B = mem.shape[0]
    out = jax.ShapeDtypeStruct((DEPTH, B, MEM_LEN, BRANCH_W), BF16)
    return pl.pallas_call(
        _mem_kv_body,
        grid=(DEPTH, B),
        in_specs=[
            pl.BlockSpec((None, MEM_LEN, D_MODEL), lambda l, b: (b, 0, 0)),
            pl.BlockSpec((None, 1, D_MODEL), lambda l, b: (l, 0, 0)),
            pl.BlockSpec((None, D_MODEL, 2 * BRANCH_W), lambda l, b: (l, 0, 0)),
        ],
        out_specs=[
            pl.BlockSpec((None, None, MEM_LEN, BRANCH_W), lambda l, b: (l, b, 0, 0)),
            pl.BlockSpec((None, None, MEM_LEN, BRANCH_W), lambda l, b: (l, b, 0, 0)),
        ],
        out_shape=[out, out],
        compiler_params=pltpu.CompilerParams(
            dimension_semantics=("arbitrary", "arbitrary"), vmem_limit_bytes=VMEM_LIMIT),
        name="mem_kv",
    )(mem, mem_norm_g.reshape(DEPTH, 1, D_MODEL), kv_w)


def _seq_proj_body(x_ref, ng_ref, w_ref, wlow_ref, gkw_ref, gkb_ref,
                   mq_ref, mk_ref, mv_ref, kmean_ref, gq_ref, gk_ref, gv_ref, gcum_ref, glow_ref):
    T = x_ref.shape[0]
    h = _rms(x_ref[...], ng_ref[...]).astype(BF16)
    low = _dot(h, wlow_ref[...]).astype(BF16)
    z = _dot(low, gkw_ref[...]) + gkb_ref[...]
    mq = _dot(h, w_ref[:, 0:512])
    mq_ref[...] = (mq * (MOBA_HD ** -0.5 * LOG2_E)).astype(BF16)
    log_sig = jnp.minimum(z, 0.0) - jnp.log(1.0 + jnp.exp(-jnp.abs(z)))
    g = log_sig / GLA_NORMALIZER
    mk = _dot(h, w_ref[:, 512:1024])
    mk_ref[...] = mk.astype(BF16)
    for j in range(T // MOBA_BLOCK):
        kmean_ref[j] = jnp.mean(mk[j * MOBA_BLOCK:(j + 1) * MOBA_BLOCK], axis=0, keepdims=True)

    C = GLA_CHUNK
    ri = lax.broadcasted_iota(jnp.int32, (C, C), 0)
    ci = lax.broadcasted_iota(jnp.int32, (C, C), 1)
    tri = jnp.where(ci <= ri, 1.0, 0.0).astype(BF16)
    lowest = None
    for c in range(T // C):
        gc = g[c * C:(c + 1) * C]
        hi = gc.astype(BF16)
        rest = gc - hi.astype(F32)
        mid = rest.astype(BF16)
        lo = (rest - mid.astype(F32)).astype(BF16)
        G = _dot(tri, hi) + _dot(tri, mid) + _dot(tri, lo)
        gcum_ref[c * C:(c + 1) * C, :] = G
        lowest = G[C - 1:C, :] if lowest is None else jnp.minimum(lowest, G[C - 1:C, :])
    glow_ref[...] = lowest
    gq_ref[...] = _dot(h, w_ref[:, 1536:1792])
    gk_ref[...] = _dot(h, w_ref[:, 1792:2048])
    gv_ref[...] = _dot(h, w_ref[:, 2048:2560]).astype(BF16)
    mv_ref[...] = _dot(h, w_ref[:, 1024:1536]).T.astype(BF16)


def _seq_proj(layer, x, norm_g, w_seq, w_low, gk_w, gk_b):
    B, S, _ = x.shape
    T = SEQ_TILE
    nblk = T // MOBA_BLOCK
    tok = lambda w: pl.BlockSpec((None, T, w), lambda b, i: (b, i, 0))
    sd = jax.ShapeDtypeStruct
    return pl.pallas_call(
        _seq_proj_body,
        grid=(B, S // T),
        in_specs=[
            tok(D_MODEL),
            _layer_resident(layer, (1, D_MODEL)),
            _layer_resident(layer, w_seq.shape[1:]),
            _layer_resident(layer, w_low.shape[1:]),
            _layer_resident(layer, gk_w.shape[1:]),
            _layer_resident(layer, (1, GLA_HEADS * GLA_DK)),
        ],
        out_specs=[
            tok(512), tok(512),
            pl.BlockSpec((None, 512, T), lambda b, i: (b, 0, i)),
            pl.BlockSpec((None, nblk, 1, 512), lambda b, i: (b, i, 0, 0)),
            tok(256), tok(256), tok(512), tok(256),
            pl.BlockSpec((None, None, 1, 256), lambda b, i: (b, i, 0, 0)),
        ],
        out_shape=[
            sd((B, S, 512), BF16), sd((B, S, 512), BF16), sd((B, 512, S), BF16),
            sd((B, S // MOBA_BLOCK, 1, 512), F32),
            sd((B, S, 256), F32), sd((B, S, 256), F32), sd((B, S, 512), BF16),
            sd((B, S, 256), F32), sd((B, S // T, 1, 256), F32),
        ],
        compiler_params=pltpu.CompilerParams(
            dimension_semantics=("arbitrary", "arbitrary"), vmem_limit_bytes=VMEM_LIMIT),
        name="seq_proj",
    )(x, norm_g.reshape(DEPTH, 1, D_MODEL), w_seq, w_low, gk_w, gk_b.reshape(DEPTH, 1, -1))


def _moba_body(q_ref, k_ref, vt_ref, kmean_ref, o_ref, vt1_ref):
    BLK = MOBA_BLOCK
    hd = MOBA_HD
    S = q_ref.shape[0]
    nb = S // BLK
    vt1_ref[0:hd, :] = vt_ref[...]
    vt1_ref[hd:, :] = jnp.ones((vt1_ref.shape[0] - hd, S), BF16)

    rs = _dot_nt(kmean_ref[...].astype(BF16), q_ref[...])
    blk = lax.broadcasted_iota(jnp.int32, (nb, S), 0)
    own = lax.broadcasted_iota(jnp.int32, (nb, S), 1) // BLK
    rs = jnp.where(blk < own, rs, NEG_INF)
    bias = jnp.full((nb, S), NEG_INF, F32)
    for _ in range(min(MOBA_TOPK, nb)):
        mx = jnp.max(rs, axis=0, keepdims=True)
        first = jnp.min(jnp.where(rs == mx, blk, nb), axis=0, keepdims=True)
        pick = (blk == first) & (mx > NEG_INF)
        bias = jnp.where(pick, 0.0, bias)
        rs = jnp.where(blk == first, NEG_INF, rs)

    key_i = lax.broadcasted_iota(jnp.int32, (BLK, BLK), 0)
    qry_i = lax.broadcasted_iota(jnp.int32, (BLK, BLK), 1)
    causal = key_i <= qry_i

    def scores(i):
        return _dot_nt(k_ref[0:(i + 1) * BLK, :], q_ref[i * BLK:(i + 1) * BLK, :])

    st_next = scores(nb - 1)
    for i in reversed(range(nb)):
        qs = slice(i * BLK, (i + 1) * BLK)
        nk = (i + 1) * BLK
        st = st_next
        if i > 0:
            st_next = scores(i - 1)
        parts =[st[n * BLK:(n + 1) * BLK] + bias[n:n + 1, qs] for n in range(i)]
        parts.append(jnp.where(causal, st[i * BLK:nk], NEG_INF))
        mx = None
        for part in parts:
            pm = jnp.max(part.reshape(BLK // SUBLANES, SUBLANES, BLK), axis=0)
            mx = pm if mx is None else jnp.maximum(mx, pm)
        m = jnp.max(mx, axis=0, keepdims=True)
        pt = jnp.concatenate([jnp.exp2(part - m).astype(BF16) for part in parts], axis=0)
        ot = _dot(vt1_ref[:, 0:nk], pt)
        o_ref[qs, :] = (ot[0:hd] / ot[hd:hd + 1]).T.astype(o_ref.dtype)


def _moba(q, k, vt, kmean):
    B, S, _ = q.shape
    nb = S // MOBA_BLOCK
    seq = pl.BlockSpec((None, S, MOBA_HD), lambda b, h: (b, 0, h))
    return pl.pallas_call(
        _moba_body,
        grid=(B, MOBA_HEADS),
        in_specs=[
            seq, seq,
            pl.BlockSpec((None, MOBA_HD, S), lambda b, h: (b, h, 0)),
            pl.BlockSpec((None, nb, MOBA_HD), lambda b, h: (b, 0, h)),
        ],
        out_specs=seq,
        out_shape=jax.ShapeDtypeStruct((B, S, BRANCH_W), BF16),
        scratch_shapes=[pltpu.VMEM((MOBA_HD + BF16_ROWS, S), BF16)],
        compiler_params=pltpu.CompilerParams(
            dimension_semantics=("arbitrary", "arbitrary"), vmem_limit_bytes=VMEM_LIMIT),
        name="moba",
    )(q, k, vt, kmean)


def _gla_body(q_ref, k_ref, v_ref, gcum_ref, glow_ref, ng_ref, o_ref, state_ref, vf_ref):
    T = q_ref.shape[0]
    C = GLA_CHUNK
    H, dk, dv = GLA_HEADS, GLA_DK, GLA_DV
    n_chunks = T // C

    @pl.when(pl.program_id(1) == 0)
    def _():
        state_ref[...] = jnp.zeros_like(state_ref)

    ri = lax.broadcasted_iota(jnp.int32, (C, C), 0)
    ci = lax.broadcasted_iota(jnp.int32, (C, C), 1)
    causal = ci <= ri
    factorable = jnp.min(glow_ref[...]) >= -GLA_SAFE_LOG_DECAY

    def run_tile(intra_fn):
        ng = ng_ref[...]
        states = [state_ref[h] for h in range(H)]
        for c in range(n_chunks):
            rows = pl.ds(c * C, C)
            G = gcum_ref[rows, :]
            G_last = G[C - 1:C, :]
            q = q_ref[rows, :] * (dk ** -0.5)
            k = k_ref[rows, :]
            q_in = (q * jnp.exp(G)).astype(BF16)
            k_st = (k * jnp.exp(G_last - G)).astype(BF16)
            a_last = jnp.exp(G_last)
            v = v_ref[rows, :]
            intra = intra_fn(c, q, k, G, q_in, v)
            for h in range(H):
                ks = slice(h * dk, (h + 1) * dk)
                vs = slice(h * dv, (h + 1) * dv)
                o = intra[h] + _dot_nt(q_in[:, ks], states[h].astype(BF16))
                states[h] = a_last[:, ks] * states[h] + _dot_tn(v[:, vs], k_st[:, ks])
                o = o * lax.rsqrt(jnp.mean(o * o, axis=-1, keepdims=True) + EPS) * ng
                o_ref[rows, vs] = o.astype(o_ref.dtype)
        for h in range(H):
            state_ref[h] = states[h]

    def intra_factored(c, q, k, G, q_in, v):
        k_out = (k * jnp.exp(-G)).astype(BF16)
        out = []
        for h in range(H):
            ks = slice(h * dk, (h + 1) * dk)
            A = jnp.where(causal, _dot_nt(q_in[:, ks], k_out[:, ks]), 0.0)
            out.append(_dot(A.astype(BF16), v[:, h * dv:(h + 1) * dv]))
        return out

    def intra_keywise(c, q, k, G, q_in, v):
        d_head = lax.broadcasted_iota(jnp.int32, (H * dk, H * dv), 0) // dk
        e_head = lax.broadcasted_iota(jnp.int32, (H * dk, H * dv), 1) // dv
        head_spread = jnp.where(d_head == e_head, 1.0, 0.0).astype(F32)
        row = lax.broadcasted_iota(jnp.int32, (C, H * dk), 0)

        def add_key(j, acc):
            r = c * C + j
            k_j = k_ref[pl.ds(r, 1), :]
            G_j = gcum_ref[pl.ds(r, 1), :]
            v_j = vf_ref[pl.ds(r, 1), :]
            w = jnp.where(row >= j, q * k_j * jnp.exp(jnp.minimum(G - G_j, 0.0)), 0.0)
            a = jnp.dot(w, head_spread, preferred_element_type=F32,
                        precision=lax.Precision.HIGHEST)
            return acc + a * v_j

        o = lax.fori_loop(0, C, add_key, jnp.zeros((C, H * dv), F32))
        return [o[:, h * dv:(h + 1) * dv] for h in range(H)]

    @pl.when(factorable)
    def _():
        run_tile(intra_factored)

    @pl.when(jnp.logical_not(factorable))
    def _():
        vf_ref[...] = v_ref[...].astype(F32)
        run_tile(intra_keywise)


def _gla(layer, q, k, v, gcum, glow, norm_g):
    B, S, _ = q.shape
    T = SEQ_TILE
    tok = lambda w: pl.BlockSpec((None, T, w), lambda b, i: (b, i, 0))
    return pl.pallas_call(
        _gla_body,
        grid=(B, S // T),
        in_specs=[tok(256), tok(256), tok(512), tok(256),
                  pl.BlockSpec((None, None, 1, 256), lambda b, i: (b, i, 0, 0)),
                  _layer_resident(layer, (1, GLA_DV))],
        out_specs=tok(512),
        out_shape=jax.ShapeDtypeStruct((B, S, BRANCH_W), BF16),
        scratch_shapes=[pltpu.VMEM((GLA_HEADS, GLA_DV, GLA_DK), F32),
                        pltpu.VMEM((T, GLA_HEADS * GLA_DV), F32)],
        compiler_params=pltpu.CompilerParams(
            dimension_semantics=("arbitrary", "arbitrary"), vmem_limit_bytes=VMEM_LIMIT),
        name="gla",
    )(q, k, v, gcum, glow, norm_g.reshape(-1, 1, GLA_DV))


def _merge_body(final, x_ref, ymoba_ref, ygla_ref, memk_ref, memv_ref, ng_ref, w_ref,
                convw_ref, convb_ref, lng_ref, lnb_ref, wb_ref, wo_ref, fg_ref,
                o_ref, u_ref):
    T = x_ref.shape[0]
    W = BRANCH_W
    x = x_ref[...]
    h = _rms(x, ng_ref[...]).astype(BF16)

    @pl.when(pl.program_id(1) == 0)
    def _():
        u_ref[0, 0:CONV_HALO, :] = jnp.zeros((CONV_HALO, W), F32)

    u_ref[0, CONV_HALO:CONV_HALO + T, :] = (
        _dot(h, w_ref[:, 0:W]) * _sigmoid(_dot(h, w_ref[:, W:2 * W])))
    span = T + CONV_HALO - SUBLANES
    for s in range(1, SUBLANES):
        u_ref[s, 0:span, :] = u_ref[0, s:s + span, :]
    R = 32
    base = CONV_HALO - (CONV_K - 1)
    conv_rows = []
    gate_logits = []
    chunks = T // R
    for r in range(chunks):
        if r % (chunks // N_BRANCH) == 0:
            g0 = 7 * W + len(gate_logits) * D_MODEL
            gate_logits.append(_dot(h, w_ref[:, g0:g0 + D_MODEL]))
        acc = jnp.zeros((R, W), F32) + convb_ref[...]
        for j in range(CONV_K):
            a, s = divmod(base + j, SUBLANES)
            lo = a * SUBLANES + r * R
            acc = acc + convw_ref[j:j + 1, :] * u_ref[s, lo:lo + R, :]
        conv_rows.append(acc)
    yc = jnp.concatenate(conv_rows, axis=0)
    u_ref[0, 0:CONV_HALO, :] = u_ref[0, T:T + CONV_HALO, :]
    mu = jnp.mean(yc, axis=-1, keepdims=True)
    yc = yc - mu
    var = jnp.mean(yc * yc, axis=-1, keepdims=True)
    yc = _silu(yc * lax.rsqrt(var + EPS) * lng_ref[...] + lnb_ref[...])
    y_conv = (yc * _silu(_dot(h, w_ref[:, 2 * W:3 * W]))).astype(BF16)

    y_moba = (ymoba_ref[...].astype(F32) * _silu(_dot(h, w_ref[:, 3 * W:4 * W]))).astype(BF16)
    y_gla = (ygla_ref[...].astype(F32) * _silu(_dot(h, w_ref[:, 4 * W:5 * W]))).astype(BF16)

    xq = (_dot(h, w_ref[:, 5 * W:6 * W]) * (MEM_HD ** -0.5)).astype(BF16)
    heads = []
    for hd in range(MEM_HEADS):
        ls = slice(hd * MEM_HD, (hd + 1) * MEM_HD)
        s = _dot_nt(xq[:, ls], memk_ref[:, ls])
        p = jnp.exp(s - jnp.max(s, axis=-1, keepdims=True))
        l = jnp.sum(p, axis=-1, keepdims=True)
        heads.append(_dot(p.astype(BF16), memv_ref[:, ls]) / l)
    y_mem = (jnp.concatenate(heads, axis=-1) * _silu(_dot(h, w_ref[:, 6 * W:7 * W]))).astype(BF16)

    merged = jnp.zeros((T, D_MODEL), F32)
    for n, y in enumerate((y_conv, y_moba, y_gla, y_mem)):
        merged = merged + _sigmoid(gate_logits[n]) * _dot(y, wb_ref[n])
    out = x + _dot(merged.astype(BF16), wo_ref[...])
    if final:
        out = _rms(out, fg_ref[...])
    o_ref[...] = out


def _merge(layer, x, y_moba, y_gla, mem_k, mem_v, norm_g, w_loc, conv_w, conv_b, ln_g, ln_b,
           w_branch, w_out, final_g):
    B, S, _ = x.shape
    T = MERGE_TILE
    final = layer == DEPTH - 1
    tok = lambda w: pl.BlockSpec((None, T, w), lambda b, i: (b, i, 0))
    mem = pl.BlockSpec((None, None, MEM_LEN, BRANCH_W), lambda b, i: (layer, b, 0, 0))
    row = lambda a: a.reshape(DEPTH, 1, -1)
    return pl.pallas_call(
        functools.partial(_merge_body, final),
        grid=(B, S // T),
        in_specs=[
            tok(D_MODEL), tok(BRANCH_W), tok(BRANCH_W), mem, mem,
            _layer_resident(layer, (1, D_MODEL)),
            _layer_resident(layer, w_loc.shape[1:]),
            _layer_resident(layer, conv_w.shape[1:]),
            _layer_resident(layer, (1, BRANCH_W)), _layer_resident(layer, (1, BRANCH_W)),
            _layer_resident(layer, (1, BRANCH_W)),
            _layer_resident(layer, w_branch.shape[1:]),
            _layer_resident(layer, w_out.shape[1:]),
            _resident((1, D_MODEL)),
        ],
        out_specs=tok(D_MODEL),
        out_shape=jax.ShapeDtypeStruct((B, S, D_MODEL), F32),
        scratch_shapes=[pltpu.VMEM((SUBLANES, T + CONV_HALO, BRANCH_W), F32)],
        compiler_params=pltpu.CompilerParams(
            dimension_semantics=("arbitrary", "arbitrary"), vmem_limit_bytes=VMEM_LIMIT),
        name="merge_final" if final else "merge",
    )(x, y_moba, y_gla, mem_k, mem_v, row(norm_g), w_loc, conv_w, row(conv_b), row(ln_g),
      row(ln_b), w_branch, w_out, final_g.reshape(1, -1))


def kernel(x, mem, norm_g, w_in, conv_w, conv_b, conv_ln_g, conv_ln_b, gla_gk_w, gla_gk_b,
           gla_norm_g, mem_norm_g, mem_kv_w, w_branch, w_out, final_norm_g):
    w_seq = jnp.concatenate([w_in[:, :, _M_Q:_M_Z], w_in[:, :, _G_Q:_G_LOW]], axis=2).astype(BF16)
    w_low = jnp.pad(w_in[:, :, _G_LOW:_G_Z], ((0, 0), (0, 0), (0, LANES - GLA_RANK))).astype(BF16)
    gk_w = jnp.pad(gla_gk_w, ((0, 0), (0, LANES - GLA_RANK), (0, 0))).astype(BF16)
    w_loc = jnp.concatenate(
        [w_in[:, :, _C_VAL:_M_Q], w_in[:, :, _M_Z:_G_Q], w_in[:, :, _G_Z:_END]], axis=2).astype(BF16)
    w_branch = w_branch.astype(BF16)
    w_out = w_out.astype(BF16)

    mem_k, mem_v = _mem_kv(mem, mem_norm_g, mem_kv_w.astype(BF16))
    for l in range(DEPTH):
        mq, mk, mv, kmean, gq, gk, gv, gcum, glow = _seq_proj(
            l, x, norm_g, w_seq, w_low, gk_w, gla_gk_b)
        y_moba = _moba(mq, mk, mv, kmean.reshape(x.shape[0], -1, BRANCH_W))
        y_gla = _gla(l, gq, gk, gv, gcum, glow, gla_norm_g)
        x = _merge(l, x, y_moba, y_gla, mem_k, mem_v, norm_g, w_loc, conv_w, conv_b,
                   conv_ln_g, conv_ln_b, w_branch, w_out, final_norm_g)
    return x
```

```python
import functools

import jax
import jax.numpy as jnp
import numpy as np
from jax import lax
from jax.experimental import pallas as pl
from jax.experimental.pallas import tpu as pltpu

D_MODEL = 1024
DEPTH = 2
MEM_LEN = 256
N_BRANCH = 4
BRANCH_W = 512
CONV_K = 31
MOBA_HEADS = 4
MOBA_HD = 128
MOBA_BLOCK = 256
MOBA_TOPK = 3
GLA_HEADS = 4
GLA_DK = 64
GLA_DV = 128
GLA_RANK = 16
GLA_NORMALIZER = 16.0
MEM_HEADS = 4
MEM_HD = 128
EPS = 1e-6

LANES = 128
SUBLANES = 8
BF16_ROWS = 2 * SUBLANES
CONV_HALO = 32
GLA_CHUNK = 128
GLA_SAFE_LOG_DECAY = 60.0
SEQ_TILE = 512
MERGE_TILE = 256
VMEM_LIMIT = 56 * 1024 * 1024

F32 = jnp.float32
BF16 = jnp.bfloat16
NEG_INF = float("-inf")
LOG2_E = 1.4426950408889634

_SPLITS = (512, 512, 512, 512, 512, 512, 512, 256, 256, 512, GLA_RANK, 512, 512, 512, 4096)
_OFF = tuple(int(o) for o in np.cumsum((0,) + _SPLITS))
(_C_VAL, _C_GLU, _C_Z, _M_Q, _M_K, _M_V, _M_Z, _G_Q, _G_K, _G_V, _G_LOW, _G_Z, _X_Q, _X_Z,
 _GATES, _END) = _OFF


def _rms(xf, g):
    return xf * lax.rsqrt(jnp.mean(xf * xf, axis=-1, keepdims=True) + EPS) * g


def _sigmoid(x):
    return 1.0 / (1.0 + jnp.exp(-x))


def _silu(x):
    return x * _sigmoid(x)


def _dot(a, b):
    return jnp.dot(a, b, preferred_element_type=F32)


def _dot_nt(a, b):
    return lax.dot_general(a, b, (((1,), (1,)), ((), ())), preferred_element_type=F32)


def _dot_tn(a, b):
    return lax.dot_general(a, b, (((0,), (0,)), ((), ())), preferred_element_type=F32)


def _resident(shape):
    nd = len(shape)
    return pl.BlockSpec(shape, lambda *_: (0,) * nd, pipeline_mode=pl.Buffered(1))


def _layer_resident(layer, shape):
    nd = len(shape)
    return pl.BlockSpec((None,) + tuple(shape), lambda *_: (layer,) + (0,) * nd,
                        pipeline_mode=pl.Buffered(1))


def _mem_kv_body(mem_ref, g_ref, w_ref, k_ref, v_ref):
    m = _rms(mem_ref[...], g_ref[...]).astype(BF16)
    kv = _dot(m, w_ref[...])
    k_ref[...] = kv[:, :BRANCH_W].astype(BF16)
    v_ref[...] = kv[:, BRANCH_W:].astype(BF16)


def _mem_kv(mem, mem_norm_g, kv_w):
    B = mem.shape[0]
    out = jax.ShapeDtypeStruct((DEPTH, B, MEM_LEN, BRANCH_W), BF16)
    return pl.pallas_call(
        _mem_kv_body,
        grid=(DEPTH, B),
        in_specs=[
            pl.BlockSpec((None, MEM_LEN, D_MODEL), lambda l, b: (b, 0, 0)),
            pl.BlockSpec((None, 1, D_MODEL), lambda l, b: (l, 0, 0)),
            pl.BlockSpec((None, D_MODEL, 2 * BRANCH_W), lambda l, b: (l, 0, 0)),
        ],
        out_specs=[
            pl.BlockSpec((None, None, MEM_LEN, BRANCH_W), lambda l, b: (l, b, 0, 0)),
            pl.BlockSpec((None, None, MEM_LEN, BRANCH_W), lambda l, b: (l, b, 0, 0)),
        ],
        out_shape=[out, out],
        compiler_params=pltpu.CompilerParams(
            dimension_semantics=("arbitrary", "arbitrary"), vmem_limit_bytes=VMEM_LIMIT),
        name="mem_kv",
    )(mem, mem_norm_g.reshape(DEPTH, 1, D_MODEL), kv_w)


def _seq_proj_body(x_ref, ng_ref, wm_ref, wg_ref, wlow_ref, gkw_ref, gkb_ref,
                   h_ref, mq_ref, mk_ref, mv_ref, kmean_ref, gq_ref, gk_ref, gv_ref, gcum_ref,
                   glow_ref):
    T = x_ref.shape[0]
    h = _rms(x_ref[...], ng_ref[...]).astype(BF16)
    h_ref[...] = h
    low = _dot(h, wlow_ref[...]).astype(BF16)
    z = _dot(low, gkw_ref[...]) + gkb_ref[...]
    mq = _dot(h, wm_ref[:, 0:512])
    mq_ref[...] = (mq * (MOBA_HD ** -0.5 * LOG2_E)).astype(BF16)
    log_sig = jnp.minimum(z, 0.0) - jnp.log(1.0 + jnp.exp(-jnp.abs(z)))
    g = log_sig / GLA_NORMALIZER
    mk = _dot(h, wm_ref[:, 512:1024])
    mk_ref[...] = mk.astype(BF16)
    for j in range(T // MOBA_BLOCK):
        kmean_ref[j] = jnp.mean(mk[j * MOBA_BLOCK:(j + 1) * MOBA_BLOCK], axis=0, keepdims=True)

    C = GLA_CHUNK
    ri = lax.broadcasted_iota(jnp.int32, (C, C), 0)
    ci = lax.broadcasted_iota(jnp.int32, (C, C), 1)
    tri = jnp.where(ci <= ri, 1.0, 0.0).astype(BF16)
    lowest = None
    for c in range(T // C):
        gc = g[c * C:(c + 1) * C]
        hi = gc.astype(BF16)
        rest = gc - hi.astype(F32)
        mid = rest.astype(BF16)
        lo = (rest - mid.astype(F32)).astype(BF16)
        G = _dot(tri, hi) + _dot(tri, mid) + _dot(tri, lo)
        gcum_ref[c * C:(c + 1) * C, :] = G
        lowest = G[C - 1:C, :] if lowest is None else jnp.minimum(lowest, G[C - 1:C, :])
    glow_ref[...] = lowest
    gq_ref[...] = _dot(h, wg_ref[:, 0:256])
    gk_ref[...] = _dot(h, wg_ref[:, 256:512])
    gv_ref[...] = _dot(h, wg_ref[:, 512:1024]).astype(BF16)
    mv_ref[...] = _dot(h, wm_ref[:, 1024:1536]).T.astype(BF16)


def _seq_proj(layer, x, norm_g, w_m, w_g, w_low, gk_w, gk_b):
    B, S, _ = x.shape
    T = SEQ_TILE
    nblk = T // MOBA_BLOCK
    tok = lambda w: pl.BlockSpec((None, T, w), lambda b, i: (b, i, 0))
    sd = jax.ShapeDtypeStruct
    return pl.pallas_call(
        _seq_proj_body,
        grid=(B, S // T),
        in_specs=[
            tok(D_MODEL),
            _layer_resident(layer, (1, D_MODEL)),
            _layer_resident(layer, w_m.shape[1:]),
            _layer_resident(layer, w_g.shape[1:]),
            _layer_resident(layer, w_low.shape[1:]),
            _layer_resident(layer, gk_w.shape[1:]),
            _layer_resident(layer, (1, GLA_HEADS * GLA_DK)),
        ],
        out_specs=[
            tok(D_MODEL), tok(512), tok(512),
            pl.BlockSpec((None, 512, T), lambda b, i: (b, 0, i)),
            pl.BlockSpec((None, nblk, 1, 512), lambda b, i: (b, i, 0, 0)),
            tok(256), tok(256), tok(512), tok(256),
            pl.BlockSpec((None, None, 1, 256), lambda b, i: (b, i, 0, 0)),
        ],
        out_shape=[
            sd((B, S, D_MODEL), BF16),
            sd((B, S, 512), BF16), sd((B, S, 512), BF16), sd((B, 512, S), BF16),
            sd((B, S // MOBA_BLOCK, 1, 512), F32),
            sd((B, S, 256), F32), sd((B, S, 256), F32), sd((B, S, 512), BF16),
            sd((B, S, 256), F32), sd((B, S // T, 1, 256), F32),
        ],
        compiler_params=pltpu.CompilerParams(
            dimension_semantics=("arbitrary", "arbitrary"), vmem_limit_bytes=VMEM_LIMIT),
        name="seq_proj",
    )(x, norm_g.reshape(DEPTH, 1, D_MODEL), w_m, w_g, w_low, gk_w, gk_b.reshape(DEPTH, 1, -1))


def _moba_body(q_ref, k_ref, vt_ref, kmean_ref, o_ref, vt1_ref):
    BLK = MOBA_BLOCK
    hd = MOBA_HD
    S = q_ref.shape[0]
    nb = S // BLK
    vt1_ref[0:hd, :] = vt_ref[...]
    vt1_ref[hd:, :] = jnp.ones((vt1_ref.shape[0] - hd, S), BF16)

    rs = _dot_nt(kmean_ref[...].astype(BF16), q_ref[...])
    blk = lax.broadcasted_iota(jnp.int32, (nb, S), 0)
    own = lax.broadcasted_iota(jnp.int32, (nb, S), 1) // BLK
    rs = jnp.where(blk < own, rs, NEG_INF)
    bias = jnp.full((nb, S), NEG_INF, F32)
    for _ in range(min(MOBA_TOPK, nb)):
        mx = jnp.max(rs, axis=0, keepdims=True)
        first = jnp.min(jnp.where(rs == mx, blk, nb), axis=0, keepdims=True)
        pick = (blk == first) & (mx > NEG_INF)
        bias = jnp.where(pick, 0.0, bias)
        rs = jnp.where(blk == first, NEG_INF, rs)

    key_i = lax.broadcasted_iota(jnp.int32, (BLK, BLK), 0)
    qry_i = lax.broadcasted_iota(jnp.int32, (BLK, BLK), 1)
    causal = key_i <= qry_i

    def scores(i):
        return _dot_nt(k_ref[0:(i + 1) * BLK, :], q_ref[i * BLK:(i + 1) * BLK, :])

    st_next = scores(nb - 1)
    for i in reversed(range(nb)):
        qs = slice(i * BLK, (i + 1) * BLK)
        nk = (i + 1) * BLK
        st = st_next
        if i > 0:
            st_next = scores(i - 1)
        parts = [st[n * BLK:(n + 1) * BLK] + bias[n:n + 1, qs] for n in range(i)]
        parts.append(jnp.where(causal, st[i * BLK:nk], NEG_INF))
        mx = None
        for part in parts:
            pm = jnp.max(part.reshape(BLK // SUBLANES, SUBLANES, BLK), axis=0)
            mx = pm if mx is None else jnp.maximum(mx, pm)
        m = jnp.max(mx, axis=0, keepdims=True)
        pt = jnp.concatenate([jnp.exp2(part - m).astype(BF16) for part in parts], axis=0)
        ot = _dot(vt1_ref[:, 0:nk], pt)
        o_ref[qs, :] = (ot[0:hd] / ot[hd:hd + 1]).T.astype(o_ref.dtype)


def _moba(q, k, vt, kmean):
    B, S, _ = q.shape
    nb = S // MOBA_BLOCK
    seq = pl.BlockSpec((None, S, MOBA_HD), lambda b, h: (b, 0, h))
    return pl.pallas_call(
        _moba_body,
        grid=(B, MOBA_HEADS),
        in_specs=[
            seq, seq,
            pl.BlockSpec((None, MOBA_HD, S), lambda b, h: (b, h, 0)),
            pl.BlockSpec((None, nb, MOBA_HD), lambda b, h: (b, 0, h)),
        ],
        out_specs=seq,
        out_shape=jax.ShapeDtypeStruct((B, S, BRANCH_W), BF16),
        scratch_shapes=[pltpu.VMEM((MOBA_HD + BF16_ROWS, S), BF16)],
        compiler_params=pltpu.CompilerParams(
            dimension_semantics=("arbitrary", "arbitrary"), vmem_limit_bytes=VMEM_LIMIT),
        name="moba",
    )(q, k, vt, kmean)


def _gla_body(q_ref, k_ref, v_ref, gcum_ref, glow_ref, ng_ref, o_ref, state_ref, vf_ref):
    T = q_ref.shape[0]
    C = GLA_CHUNK
    H, dk, dv = GLA_HEADS, GLA_DK, GLA_DV
    n_chunks = T // C

    @pl.when(pl.program_id(1) == 0)
    def _():
        state_ref[...] = jnp.zeros_like(state_ref)

    ri = lax.broadcasted_iota(jnp.int32, (C, C), 0)
    ci = lax.broadcasted_iota(jnp.int32, (C, C), 1)
    causal = ci <= ri
    factorable = jnp.min(glow_ref[...]) >= -GLA_SAFE_LOG_DECAY

    def run_tile(intra_fn):
        ng = ng_ref[...]
        states = [state_ref[h] for h in range(H)]
        for c in range(n_chunks):
            rows = pl.ds(c * C, C)
            G = gcum_ref[rows, :]
            G_last = G[C - 1:C, :]
            q = q_ref[rows, :] * (dk ** -0.5)
            k = k_ref[rows, :]
            q_in = (q * jnp.exp(G)).astype(BF16)
            k_st = (k * jnp.exp(G_last - G)).astype(BF16)
            a_last = jnp.exp(G_last)
            v = v_ref[rows, :]
            intra = intra_fn(c, q, k, G, q_in, v)
            for h in range(H):
                ks = slice(h * dk, (h + 1) * dk)
                vs = slice(h * dv, (h + 1) * dv)
                o = intra[h] + _dot_nt(q_in[:, ks], states[h].astype(BF16))
                states[h] = a_last[:, ks] * states[h] + _dot_tn(v[:, vs], k_st[:, ks])
                o = o * lax.rsqrt(jnp.mean(o * o, axis=-1, keepdims=True) + EPS) * ng
                o_ref[rows, vs] = o.astype(o_ref.dtype)
        for h in range(H):
            state_ref[h] = states[h]

    def intra_factored(c, q, k, G, q_in, v):
        k_out = (k * jnp.exp(-G)).astype(BF16)
        out = []
        for h in range(H):
            ks = slice(h * dk, (h + 1) * dk)
            A = jnp.where(causal, _dot_nt(q_in[:, ks], k_out[:, ks]), 0.0)
            out.append(_dot(A.astype(BF16), v[:, h * dv:(h + 1) * dv]))
        return out

    def intra_keywise(c, q, k, G, q_in, v):
        d_head = lax.broadcasted_iota(jnp.int32, (H * dk, H * dv), 0) // dk
        e_head = lax.broadcasted_iota(jnp.int32, (H * dk, H * dv), 1) // dv
        head_spread = jnp.where(d_head == e_head, 1.0, 0.0).astype(F32)
        row = lax.broadcasted_iota(jnp.int32, (C, H * dk), 0)

        def add_key(j, acc):
            r = c * C + j
            k_j = k_ref[pl.ds(r, 1), :]
            G_j = gcum_ref[pl.ds(r, 1), :]
            v_j = vf_ref[pl.ds(r, 1), :]
            w = jnp.where(row >= j, q * k_j * jnp.exp(jnp.minimum(G - G_j, 0.0)), 0.0)
            a = jnp.dot(w, head_spread, preferred_element_type=F32,
                        precision=lax.Precision.HIGHEST)
            return acc + a * v_j

        o = lax.fori_loop(0, C, add_key, jnp.zeros((C, H * dv), F32))
        return [o[:, h * dv:(h + 1) * dv] for h in range(H)]

    @pl.when(factorable)
    def _():
        run_tile(intra_factored)

    @pl.when(jnp.logical_not(factorable))
    def _():
        vf_ref[...] = v_ref[...].astype(F32)
        run_tile(intra_keywise)


def _gla(layer, q, k, v, gcum, glow, norm_g):
    B, S, _ = q.shape
    T = SEQ_TILE
    tok = lambda w: pl.BlockSpec((None, T, w), lambda b, i: (b, i, 0))
    return pl.pallas_call(
        _gla_body,
        grid=(B, S // T),
        in_specs=[tok(256), tok(256), tok(512), tok(256),
                  pl.BlockSpec((None, None, 1, 256), lambda b, i: (b, i, 0, 0)),
                  _layer_resident(layer, (1, GLA_DV))],
        out_specs=tok(512),
        out_shape=jax.ShapeDtypeStruct((B, S, BRANCH_W), BF16),
        scratch_shapes=[pltpu.VMEM((GLA_HEADS, GLA_DV, GLA_DK), F32),
                        pltpu.VMEM((T, GLA_HEADS * GLA_DV), F32)],
        compiler_params=pltpu.CompilerParams(
            dimension_semantics=("arbitrary", "arbitrary"), vmem_limit_bytes=VMEM_LIMIT),
        name="gla",
    )(q, k, v, gcum, glow, norm_g.reshape(-1, 1, GLA_DV))


def _merge_body(final, x_ref, h_ref, ymoba_ref, ygla_ref, memk_ref, memv_ref, wc_ref, wmz_ref,
                wr_ref, convw_ref, convb_ref, lng_ref, lnb_ref, wb_ref, wo_ref, fg_ref,
                o_ref, u_ref):
    T = x_ref.shape[0]
    W = BRANCH_W

    @pl.when(pl.program_id(1) == 0)
    def _():
        u_ref[0, 0:CONV_HALO, :] = jnp.zeros((CONV_HALO, W), F32)

    h = h_ref[...]
    u_ref[0, CONV_HALO:CONV_HALO + T, :] = (
        _dot(h, wc_ref[:, 0:W]) * _sigmoid(_dot(h, wc_ref[:, W:2 * W])))
    span = T + CONV_HALO - SUBLANES
    for s in range(1, SUBLANES):
        u_ref[s, 0:span, :] = u_ref[0, s:s + span, :]
    R = 32
    base = CONV_HALO - (CONV_K - 1)
    conv_rows = []
    gate_logits = []
    chunks = T // R
    for r in range(chunks):
        if r % (chunks // N_BRANCH) == 0:
            g0 = 3 * W + len(gate_logits) * D_MODEL
            gate_logits.append(_dot(h, wr_ref[:, g0:g0 + D_MODEL]))
        acc = jnp.zeros((R, W), F32) + convb_ref[...]
        for j in range(CONV_K):
            a, s = divmod(base + j, SUBLANES)
            lo = a * SUBLANES + r * R
            acc = acc + convw_ref[j:j + 1, :] * u_ref[s, lo:lo + R, :]
        conv_rows.append(acc)
    yc = jnp.concatenate(conv_rows, axis=0)
    u_ref[0, 0:CONV_HALO, :] = u_ref[0, T:T + CONV_HALO, :]
    mu = jnp.mean(yc, axis=-1, keepdims=True)
    yc = yc - mu
    var = jnp.mean(yc * yc, axis=-1, keepdims=True)
    yc = _silu(yc * lax.rsqrt(var + EPS) * lng_ref[...] + lnb_ref[...])
    y_conv = (yc * _silu(_dot(h, wc_ref[:, 2 * W:3 * W]))).astype(BF16)

    y_moba = (ymoba_ref[...].astype(F32) * _silu(_dot(h, wmz_ref[...]))).astype(BF16)
    y_gla = (ygla_ref[...].astype(F32) * _silu(_dot(h, wr_ref[:, 0:W]))).astype(BF16)

    xq = (_dot(h, wr_ref[:, W:2 * W]) * (MEM_HD ** -0.5)).astype(BF16)
    heads = []
    for hd in range(MEM_HEADS):
        ls = slice(hd * MEM_HD, (hd + 1) * MEM_HD)
        s = _dot_nt(xq[:, ls], memk_ref[:, ls])
        p = jnp.exp(s - jnp.max(s, axis=-1, keepdims=True))
        l = jnp.sum(p, axis=-1, keepdims=True)
        heads.append(_dot(p.astype(BF16), memv_ref[:, ls]) / l)
    y_mem = (jnp.concatenate(heads, axis=-1) * _silu(_dot(h, wr_ref[:, 2 * W:3 * W]))).astype(BF16)

    merged = jnp.zeros((T, D_MODEL), F32)
    for n, y in enumerate((y_conv, y_moba, y_gla, y_mem)):
        merged = merged + _sigmoid(gate_logits[n]) * _dot(y, wb_ref[n])
    out = x_ref[...] + _dot(merged.astype(BF16), wo_ref[...])
    if final:
        out = _rms(out, fg_ref[...])
    o_ref[...] = out


def _merge(layer, x, h, y_moba, y_gla, mem_k, mem_v, w_c, w_mz, w_r, conv_w, conv_b, ln_g, ln_b,
           w_branch, w_out, final_g):
    B, S, _ = x.shape
    T = MERGE_TILE
    final = layer == DEPTH - 1
    tok = lambda w: pl.BlockSpec((None, T, w), lambda b, i: (b, i, 0))
    mem = pl.BlockSpec((None, None, MEM_LEN, BRANCH_W), lambda b, i: (layer, b, 0, 0))
    row = lambda a: a.reshape(DEPTH, 1, -1)
    return pl.pallas_call(
        functools.partial(_merge_body, final),
        grid=(B, S // T),
        in_specs=[
            tok(D_MODEL), tok(D_MODEL), tok(BRANCH_W), tok(BRANCH_W), mem, mem,
            _layer_resident(layer, w_c.shape[1:]),
            _layer_resident(layer, w_mz.shape[1:]),
            _layer_resident(layer, w_r.shape[1:]),
            _layer_resident(layer, conv_w.shape[1:]),
            _layer_resident(layer, (1, BRANCH_W)), _layer_resident(layer, (1, BRANCH_W)),
            _layer_resident(layer, (1, BRANCH_W)),
            _layer_resident(layer, w_branch.shape[1:]),
            _layer_resident(layer, w_out.shape[1:]),
            _resident((1, D_MODEL)),
        ],
        out_specs=tok(D_MODEL),
        out_shape=jax.ShapeDtypeStruct((B, S, D_MODEL), F32),
        scratch_shapes=[pltpu.VMEM((SUBLANES, T + CONV_HALO, BRANCH_W), F32)],
        compiler_params=pltpu.CompilerParams(
            dimension_semantics=("arbitrary", "arbitrary"), vmem_limit_bytes=VMEM_LIMIT),
        name="merge_final" if final else "merge",
    )(x, h, y_moba, y_gla, mem_k, mem_v, w_c, w_mz, w_r, conv_w, row(conv_b), row(ln_g),
      row(ln_b), w_branch, w_out, final_g.reshape(1, -1))


def kernel(x, mem, norm_g, w_in, conv_w, conv_b, conv_ln_g, conv_ln_b, gla_gk_w, gla_gk_b,
           gla_norm_g, mem_norm_g, mem_kv_w, w_branch, w_out, final_norm_g):
    w_m = w_in[:, :, _M_Q:_M_Z].astype(BF16)
    w_g = w_in[:, :, _G_Q:_G_LOW].astype(BF16)
    w_low = jnp.pad(w_in[:, :, _G_LOW:_G_Z], ((0, 0), (0, 0), (0, LANES - GLA_RANK))).astype(BF16)
    gk_w = jnp.pad(gla_gk_w, ((0, 0), (0, LANES - GLA_RANK), (0, 0))).astype(BF16)
    w_c = w_in[:, :, _C_VAL:_M_Q].astype(BF16)
    w_mz = w_in[:, :, _M_Z:_G_Q].astype(BF16)
    w_r = w_in[:, :, _G_Z:_END].astype(BF16)
    w_branch = w_branch.astype(BF16)
    w_out = w_out.astype(BF16)

    mem_k, mem_v = _mem_kv(mem, mem_norm_g, mem_kv_w.astype(BF16))
    for l in range(DEPTH):
        h, mq, mk, mv, kmean, gq, gk, gv, gcum, glow = _seq_proj(
            l, x, norm_g, w_m, w_g, w_low, gk_w, gla_gk_b)
        y_moba = _moba(mq, mk, mv, kmean.reshape(x.shape[0], -1, BRANCH_W))
        y_gla = _gla(l, gq, gk, gv, gcum, glow, gla_norm_g)
        x = _merge(l, x, h, y_moba, y_gla, mem_k, mem_v, w_c, w_mz, w_r, conv_w, conv_b,
                   conv_ln_g, conv_ln_b, w_branch, w_out, final_norm_g)
    return x
```

```python
import functools

import jax
import jax.numpy as jnp
import numpy as np
from jax import lax
from jax.experimental import pallas as pl
from jax.experimental.pallas import tpu as pltpu

D_MODEL = 1024
DEPTH = 2
MEM_LEN = 256
N_BRANCH = 4
BRANCH_W = 512
CONV_K = 31
MOBA_HEADS = 4
MOBA_HD = 128
MOBA_BLOCK = 256
MOBA_TOPK = 3
GLA_HEADS = 4
GLA_DK = 64
GLA_DV = 128
GLA_RANK = 16
GLA_NORMALIZER = 16.0
MEM_HEADS = 4
MEM_HD = 128
EPS = 1e-6

LANES = 128
SUBLANES = 8
BF16_ROWS = 2 * SUBLANES
CONV_HALO = 32
GLA_CHUNK = 128
GLA_SAFE_LOG_DECAY = 60.0
SEQ_TILE = 1024
GLA_TILE = 512
MERGE_TILE = 256
VMEM_LIMIT = 56 * 1024 * 1024

F32 = jnp.float32
BF16 = jnp.bfloat16
NEG_INF = float("-inf")
LOG2_E = 1.4426950408889634

_SPLITS = (512, 512, 512, 512, 512, 512, 512, 256, 256, 512, GLA_RANK, 512, 512, 512, 4096)
_OFF = tuple(int(o) for o in np.cumsum((0,) + _SPLITS))
(_C_VAL, _C_GLU, _C_Z, _M_Q, _M_K, _M_V, _M_Z, _G_Q, _G_K, _G_V, _G_LOW, _G_Z, _X_Q, _X_Z,
 _GATES, _END) = _OFF


def _rms(xf, g):
    return xf * lax.rsqrt(jnp.mean(xf * xf, axis=-1, keepdims=True) + EPS) * g


def _sigmoid(x):
    return 1.0 / (1.0 + jnp.exp(-x))


def _silu(x):
    return x * _sigmoid(x)


def _dot(a, b):
    return jnp.dot(a, b, preferred_element_type=F32)


def _dot_nt(a, b):
    return lax.dot_general(a, b, (((1,), (1,)), ((), ())), preferred_element_type=F32)


def _dot_tn(a, b):
    return lax.dot_general(a, b, (((0,), (0,)), ((), ())), preferred_element_type=F32)


def _resident(shape):
    nd = len(shape)
    return pl.BlockSpec(shape, lambda *_: (0,) * nd, pipeline_mode=pl.Buffered(1))


def _layer_resident(layer, shape):
    nd = len(shape)
    return pl.BlockSpec((None,) + tuple(shape), lambda *_: (layer,) + (0,) * nd,
                        pipeline_mode=pl.Buffered(1))


def _mem_kv_body(mem_ref, g_ref, w_ref, k_ref, v_ref):
    m = _rms(mem_ref[...], g_ref[...]).astype(BF16)
    kv = _dot(m, w_ref[...])
    k_ref[...] = kv[:, :BRANCH_W].astype(BF16)
    v_ref[...] = kv[:, BRANCH_W:].astype(BF16)


def _mem_kv(mem, mem_norm_g, kv_w):
    B = mem.shape[0]
    out = jax.ShapeDtypeStruct((DEPTH, B, MEM_LEN, BRANCH_W), BF16)
    return pl.pallas_call(
        _mem_kv_body,
        grid=(DEPTH, B),
        in_specs=[
            pl.BlockSpec((None, MEM_LEN, D_MODEL), lambda l, b: (b, 0, 0)),
            pl.BlockSpec((None, 1, D_MODEL), lambda l, b: (l, 0, 0)),
            pl.BlockSpec((None, D_MODEL, 2 * BRANCH_W), lambda l, b: (l, 0, 0)),
        ],
        out_specs=[
            pl.BlockSpec((None, None, MEM_LEN, BRANCH_W), lambda l, b: (l, b, 0, 0)),
            pl.BlockSpec((None, None, MEM_LEN, BRANCH_W), lambda l, b: (l, b, 0, 0)),
        ],
        out_shape=[out, out],
        compiler_params=pltpu.CompilerParams(
            dimension_semantics=("arbitrary", "arbitrary"), vmem_limit_bytes=VMEM_LIMIT),
        name="mem_kv",
    )(mem, mem_norm_g.reshape(DEPTH, 1, D_MODEL), kv_w)


def _seq_proj_body(x_ref, ng_ref, wm_ref, wg_ref, wlow_ref, gkw_ref, gkb_ref,
                   h_ref, mq_ref, mk_ref, mv_ref, kmean_ref, gq_ref, gk_ref, gv_ref, gcum_ref,
                   glow_ref):
    T = x_ref.shape[0]
    h = _rms(x_ref[...], ng_ref[...]).astype(BF16)
    h_ref[...] = h
    low = _dot(h, wlow_ref[...]).astype(BF16)
    z = _dot(low, gkw_ref[...]) + gkb_ref[...]
    mq = _dot(h, wm_ref[:, 0:512])
    mq_ref[...] = (mq * (MOBA_HD ** -0.5 * LOG2_E)).astype(BF16)
    log_sig = jnp.minimum(z, 0.0) - jnp.log(1.0 + jnp.exp(-jnp.abs(z)))
    g = log_sig / GLA_NORMALIZER
    mk = _dot(h, wm_ref[:, 512:1024])
    mk_ref[...] = mk.astype(BF16)
    for j in range(T // MOBA_BLOCK):
        kmean_ref[j] = jnp.mean(mk[j * MOBA_BLOCK:(j + 1) * MOBA_BLOCK], axis=0, keepdims=True)

    C = GLA_CHUNK
    ri = lax.broadcasted_iota(jnp.int32, (C, C), 0)
    ci = lax.broadcasted_iota(jnp.int32, (C, C), 1)
    tri = jnp.where(ci <= ri, 1.0, 0.0).astype(BF16)
    lowest = None
    for c in range(T // C):
        gc = g[c * C:(c + 1) * C]
        hi = gc.astype(BF16)
        rest = gc - hi.astype(F32)
        mid = rest.astype(BF16)
        lo = (rest - mid.astype(F32)).astype(BF16)
        G = _dot(tri, hi) + _dot(tri, mid) + _dot(tri, lo)
        gcum_ref[c * C:(c + 1) * C, :] = G
        lowest = G[C - 1:C, :] if lowest is None else jnp.minimum(lowest, G[C - 1:C, :])
    glow_ref[...] = lowest
    gq_ref[...] = _dot(h, wg_ref[:, 0:256])
    gk_ref[...] = _dot(h, wg_ref[:, 256:512])
    gv_ref[...] = _dot(h, wg_ref[:, 512:1024]).astype(BF16)
    mv_ref[...] = _dot(h, wm_ref[:, 1024:1536]).T.astype(BF16)


def _seq_proj(layer, x, norm_g, w_m, w_g, w_low, gk_w, gk_b):
    B, S, _ = x.shape
    T = SEQ_TILE
    nblk = T // MOBA_BLOCK
    tok = lambda w: pl.BlockSpec((None, T, w), lambda b, i: (b, i, 0))
    sd = jax.ShapeDtypeStruct
    return pl.pallas_call(
        _seq_proj_body,
        grid=(B, S // T),
        in_specs=[
            tok(D_MODEL),
            _layer_resident(layer, (1, D_MODEL)),
            _layer_resident(layer, w_m.shape[1:]),
            _layer_resident(layer, w_g.shape[1:]),
            _layer_resident(layer, w_low.shape[1:]),
            _layer_resident(layer, gk_w.shape[1:]),
            _layer_resident(layer, (1, GLA_HEADS * GLA_DK)),
        ],
        out_specs=[
            tok(D_MODEL), tok(512), tok(512),
            pl.BlockSpec((None, 512, T), lambda b, i: (b, 0, i)),
            pl.BlockSpec((None, nblk, 1, 512), lambda b, i: (b, i, 0, 0)),
            tok(256), tok(256), tok(512), tok(256),
            pl.BlockSpec((None, None, 1, 256), lambda b, i: (b, i, 0, 0)),
        ],
        out_shape=[
            sd((B, S, D_MODEL), BF16),
            sd((B, S, 512), BF16), sd((B, S, 512), BF16), sd((B, 512, S), BF16),
            sd((B, S // MOBA_BLOCK, 1, 512), F32),
            sd((B, S, 256), F32), sd((B, S, 256), F32), sd((B, S, 512), BF16),
            sd((B, S, 256), F32), sd((B, S // T, 1, 256), F32),
        ],
        compiler_params=pltpu.CompilerParams(
            dimension_semantics=("arbitrary", "arbitrary"), vmem_limit_bytes=VMEM_LIMIT),
        name="seq_proj",
    )(x, norm_g.reshape(DEPTH, 1, D_MODEL), w_m, w_g, w_low, gk_w, gk_b.reshape(DEPTH, 1, -1))


def _moba_body(q_ref, k_ref, vt_ref, kmean_ref, o_ref, vt1_ref):
    BLK = MOBA_BLOCK
    hd = MOBA_HD
    S = q_ref.shape[0]
    nb = S // BLK
    vt1_ref[0:hd, :] = vt_ref[...]
    vt1_ref[hd:, :] = jnp.ones((vt1_ref.shape[0] - hd, S), BF16)

    rs = _dot_nt(kmean_ref[...].astype(BF16), q_ref[...])
    blk = lax.broadcasted_iota(jnp.int32, (nb, S), 0)
    own = lax.broadcasted_iota(jnp.int32, (nb, S), 1) // BLK
    rs = jnp.where(blk < own, rs, NEG_INF)
    bias = jnp.full((nb, S), NEG_INF, F32)
    for _ in range(min(MOBA_TOPK, nb)):
        mx = jnp.max(rs, axis=0, keepdims=True)
        first = jnp.min(jnp.where(rs == mx, blk, nb), axis=0, keepdims=True)
        pick = (blk == first) & (mx > NEG_INF)
        bias = jnp.where(pick, 0.0, bias)
        rs = jnp.where(blk == first, NEG_INF, rs)

    key_i = lax.broadcasted_iota(jnp.int32, (BLK, BLK), 0)
    qry_i = lax.broadcasted_iota(jnp.int32, (BLK, BLK), 1)
    causal = key_i <= qry_i

    def scores(i):
        return _dot_nt(k_ref[0:(i + 1) * BLK, :], q_ref[i * BLK:(i + 1) * BLK, :])

    st_next = scores(nb - 1)
    for i in reversed(range(nb)):
        qs = slice(i * BLK, (i + 1) * BLK)
        nk = (i + 1) * BLK
        st = st_next
        if i > 0:
            st_next = scores(i - 1)
        parts = [st[n * BLK:(n + 1) * BLK] + bias[n:n + 1, qs] for n in range(i)]
        parts.append(jnp.where(causal, st[i * BLK:nk], NEG_INF))
        mx = None
        for part in parts:
            pm = jnp.max(part.reshape(BLK // SUBLANES, SUBLANES, BLK), axis=0)
            mx = pm if mx is None else jnp.maximum(mx, pm)
        m = jnp.max(mx, axis=0, keepdims=True)
        pt = jnp.concatenate([jnp.exp2(part - m).astype(BF16) for part in parts], axis=0)
        ot = _dot(vt1_ref[:, 0:nk], pt)
        o_ref[qs, :] = (ot[0:hd] / ot[hd:hd + 1]).T.astype(o_ref.dtype)


def _moba(q, k, vt, kmean):
    B, S, _ = q.shape
    nb = S // MOBA_BLOCK
    seq = pl.BlockSpec((None, S, MOBA_HD), lambda b, h: (b, 0, h))
    return pl.pallas_call(
        _moba_body,
        grid=(B, MOBA_HEADS),
        in_specs=[
            seq, seq,
            pl.BlockSpec((None, MOBA_HD, S), lambda b, h: (b, h, 0)),
            pl.BlockSpec((None, nb, MOBA_HD), lambda b, h: (b, 0, h)),
        ],
        out_specs=seq,
        out_shape=jax.ShapeDtypeStruct((B, S, BRANCH_W), BF16),
        scratch_shapes=[pltpu.VMEM((MOBA_HD + BF16_ROWS, S), BF16)],
        compiler_params=pltpu.CompilerParams(
            dimension_semantics=("arbitrary", "arbitrary"), vmem_limit_bytes=VMEM_LIMIT),
        name="moba",
    )(q, k, vt, kmean)


def _gla_body(q_ref, k_ref, v_ref, gcum_ref, glow_ref, ng_ref, o_ref, state_ref, vf_ref):
    T = q_ref.shape[0]
    C = GLA_CHUNK
    H, dk, dv = GLA_HEADS, GLA_DK, GLA_DV
    n_chunks = T // C

    @pl.when(pl.program_id(1) == 0)
    def _():
        state_ref[...] = jnp.zeros_like(state_ref)

    ri = lax.broadcasted_iota(jnp.int32, (C, C), 0)
    ci = lax.broadcasted_iota(jnp.int32, (C, C), 1)
    causal = ci <= ri
    factorable = jnp.min(glow_ref[...]) >= -GLA_SAFE_LOG_DECAY

    def run_tile(intra_fn):
        ng = ng_ref[...]
        pending = []
        for c in range(n_chunks):
            rows = pl.ds(c * C, C)
            G = gcum_ref[rows, :]
            G_last = G[C - 1:C, :]
            q = q_ref[rows, :] * (dk ** -0.5)
            k = k_ref[rows, :]
            q_in = (q * jnp.exp(G)).astype(BF16)
            k_st = (k * jnp.exp(G_last - G)).astype(BF16)
            a_last = jnp.exp(G_last)
            v = v_ref[rows, :]
            intra = intra_fn(c, q, k, G, q_in, v)
            kv = [_dot_tn(v[:, h * dv:(h + 1) * dv], k_st[:, h * dk:(h + 1) * dk])
                  for h in range(H)]
            pending.append((q_in, a_last, intra, kv))
        states = [state_ref[h] for h in range(H)]
        for c, (q_in, a_last, intra, kv) in enumerate(pending):
            rows = pl.ds(c * C, C)
            for h in range(H):
                ks = slice(h * dk, (h + 1) * dk)
                vs = slice(h * dv, (h + 1) * dv)
                o = intra[h] + _dot_nt(q_in[:, ks], states[h].astype(BF16))
                states[h] = a_last[:, ks] * states[h] + kv[h]
                o = o * lax.rsqrt(jnp.mean(o * o, axis=-1, keepdims=True) + EPS) * ng
                o_ref[rows, vs] = o.astype(o_ref.dtype)
        for h in range(H):
            state_ref[h] = states[h]

    def intra_factored(c, q, k, G, q_in, v):
        k_out = (k * jnp.exp(-G)).astype(BF16)
        out = []
        for h in range(H):
            ks = slice(h * dk, (h + 1) * dk)
            A = jnp.where(causal, _dot_nt(q_in[:, ks], k_out[:, ks]), 0.0)
            out.append(_dot(A.astype(BF16), v[:, h * dv:(h + 1) * dv]))
        return out

    def intra_keywise(c, q, k, G, q_in, v):
        d_head = lax.broadcasted_iota(jnp.int32, (H * dk, H * dv), 0) // dk
        e_head = lax.broadcasted_iota(jnp.int32, (H * dk, H * dv), 1) // dv
        head_spread = jnp.where(d_head == e_head, 1.0, 0.0).astype(F32)
        row = lax.broadcasted_iota(jnp.int32, (C, H * dk), 0)

        def add_key(j, acc):
            r = c * C + j
            k_j = k_ref[pl.ds(r, 1), :]
            G_j = gcum_ref[pl.ds(r, 1), :]
            v_j = vf_ref[pl.ds(r, 1), :]
            w = jnp.where(row >= j, q * k_j * jnp.exp(jnp.minimum(G - G_j, 0.0)), 0.0)
            a = jnp.dot(w, head_spread, preferred_element_type=F32,
                        precision=lax.Precision.HIGHEST)
            return acc + a * v_j

        o = lax.fori_loop(0, C, add_key, jnp.zeros((C, H * dv), F32))
        return [o[:, h * dv:(h + 1) * dv] for h in range(H)]

    @pl.when(factorable)
    def _():
        run_tile(intra_factored)

    @pl.when(jnp.logical_not(factorable))
    def _():
        vf_ref[...] = v_ref[...].astype(F32)
        run_tile(intra_keywise)


def _gla(layer, q, k, v, gcum, glow, norm_g):
    B, S, _ = q.shape
    T = GLA_TILE
    per_guard = SEQ_TILE // GLA_TILE
    tok = lambda w: pl.BlockSpec((None, T, w), lambda b, i: (b, i, 0))
    return pl.pallas_call(
        _gla_body,
        grid=(B, S // T),
        in_specs=[tok(256), tok(256), tok(512), tok(256),
                  pl.BlockSpec((None, None, 1, 256), lambda b, i: (b, i // per_guard, 0, 0)),
                  _layer_resident(layer, (1, GLA_DV))],
        out_specs=tok(512),
        out_shape=jax.ShapeDtypeStruct((B, S, BRANCH_W), BF16),
        scratch_shapes=[pltpu.VMEM((GLA_HEADS, GLA_DV, GLA_DK), F32),
                        pltpu.VMEM((T, GLA_HEADS * GLA_DV), F32)],
        compiler_params=pltpu.CompilerParams(
            dimension_semantics=("arbitrary", "arbitrary"), vmem_limit_bytes=VMEM_LIMIT),
        name="gla",
    )(q, k, v, gcum, glow, norm_g.reshape(-1, 1, GLA_DV))


def _merge_body(final, x_ref, h_ref, ymoba_ref, ygla_ref, memk_ref, memv_ref, wc_ref, wmz_ref,
                wr_ref, convw_ref, convb_ref, lng_ref, lnb_ref, wb_ref, wo_ref, fg_ref,
                o_ref, u_ref):
    T = x_ref.shape[0]
    W = BRANCH_W

    @pl.when(pl.program_id(1) == 0)
    def _():
        u_ref[0, 0:CONV_HALO, :] = jnp.zeros((CONV_HALO, W), F32)

    h = h_ref[...]

    def proj(w):
        return _dot(h, w)

    u_ref[0, CONV_HALO:CONV_HALO + T, :] = (
        proj(wc_ref[:, 0:W]) * _sigmoid(proj(wc_ref[:, W:2 * W])))
    span = T + CONV_HALO - SUBLANES
    for s in range(1, SUBLANES):
        u_ref[s, 0:span, :] = u_ref[0, s:s + span, :]
    R = 32
    base = CONV_HALO - (CONV_K - 1)
    conv_rows = []
    gate_logits = []
    chunks = T // R
    for r in range(chunks):
        if r % (chunks // N_BRANCH) == 0:
            g0 = 3 * W + len(gate_logits) * D_MODEL
            gate_logits.append(proj(wr_ref[:, g0:g0 + D_MODEL]))
        acc = jnp.zeros((R, W), F32) + convb_ref[...]
        for j in range(CONV_K):
            a, s = divmod(base + j, SUBLANES)
            lo = a * SUBLANES + r * R
            acc = acc + convw_ref[j:j + 1, :] * u_ref[s, lo:lo + R, :]
        conv_rows.append(acc)
    yc = jnp.concatenate(conv_rows, axis=0)
    u_ref[0, 0:CONV_HALO, :] = u_ref[0, T:T + CONV_HALO, :]
    mu = jnp.mean(yc, axis=-1, keepdims=True)
    yc = yc - mu
    var = jnp.mean(yc * yc, axis=-1, keepdims=True)
    yc = _silu(yc * lax.rsqrt(var + EPS) * lng_ref[...] + lnb_ref[...])
    y_conv = (yc * _silu(proj(wc_ref[:, 2 * W:3 * W]))).astype(BF16)

    y_moba = (ymoba_ref[...].astype(F32) * _silu(proj(wmz_ref[...]))).astype(BF16)
    y_gla = (ygla_ref[...].astype(F32) * _silu(proj(wr_ref[:, 0:W]))).astype(BF16)

    xq = (proj(wr_ref[:, W:2 * W]) * (MEM_HD ** -0.5)).astype(BF16)
    heads = []
    for hd in range(MEM_HEADS):
        ls = slice(hd * MEM_HD, (hd + 1) * MEM_HD)
        s = _dot_nt(xq[:, ls], memk_ref[:, ls])
        p = jnp.exp(s - jnp.max(s, axis=-1, keepdims=True))
        l = jnp.sum(p, axis=-1, keepdims=True)
        heads.append(_dot(p.astype(BF16), memv_ref[:, ls]) / l)
    y_mem = (jnp.concatenate(heads, axis=-1) * _silu(proj(wr_ref[:, 2 * W:3 * W]))).astype(BF16)

    merged = jnp.zeros((T, D_MODEL), F32)
    for n, y in enumerate((y_conv, y_moba, y_gla, y_mem)):
        merged = merged + _sigmoid(gate_logits[n]) * _dot(y, wb_ref[n])
    out = x_ref[...] + _dot(merged.astype(BF16), wo_ref[...])
    if final:
        out = _rms(out, fg_ref[...])
    o_ref[...] = out


def _merge(layer, x, h, y_moba, y_gla, mem_k, mem_v, w_c, w_mz, w_r, conv_w, conv_b, ln_g, ln_b,
           w_branch, w_out, final_g):
    B, S, _ = x.shape
    T = MERGE_TILE
    final = layer == DEPTH - 1
    tok = lambda w: pl.BlockSpec((None, T, w), lambda b, i: (b, i, 0))
    mem = pl.BlockSpec((None, None, MEM_LEN, BRANCH_W), lambda b, i: (layer, b, 0, 0))
    row = lambda a: a.reshape(DEPTH, 1, -1)
    return pl.pallas_call(
        functools.partial(_merge_body, final),
        grid=(B, S // T),
        in_specs=[
            tok(D_MODEL), tok(D_MODEL), tok(BRANCH_W), tok(BRANCH_W), mem, mem,
            _layer_resident(layer, w_c.shape[1:]),
            _layer_resident(layer, w_mz.shape[1:]),
            _layer_resident(layer, w_r.shape[1:]),
            _layer_resident(layer, conv_w.shape[1:]),
            _layer_resident(layer, (1, BRANCH_W)), _layer_resident(layer, (1, BRANCH_W)),
            _layer_resident(layer, (1, BRANCH_W)),
            _layer_resident(layer, w_branch.shape[1:]),
            _layer_resident(layer, w_out.shape[1:]),
            _resident((1, D_MODEL)),
        ],
        out_specs=tok(D_MODEL),
        out_shape=jax.ShapeDtypeStruct((B, S, D_MODEL), F32),
        scratch_shapes=[pltpu.VMEM((SUBLANES, T + CONV_HALO, BRANCH_W), F32)],
        compiler_params=pltpu.CompilerParams(
            dimension_semantics=("arbitrary", "arbitrary"), vmem_limit_bytes=VMEM_LIMIT),
        name="merge_final" if final else "merge",
    )(x, h, y_moba, y_gla, mem_k, mem_v, w_c, w_mz, w_r, conv_w, row(conv_b), row(ln_g),
      row(ln_b), w_branch, w_out, final_g.reshape(1, -1))


def kernel(x, mem, norm_g, w_in, conv_w, conv_b, conv_ln_g, conv_ln_b, gla_gk_w, gla_gk_b,
           gla_norm_g, mem_norm_g, mem_kv_w, w_branch, w_out, final_norm_g):
    w_m = w_in[:, :, _M_Q:_M_Z].astype(BF16)
    w_g = w_in[:, :, _G_Q:_G_LOW].astype(BF16)
    w_low = jnp.pad(w_in[:, :, _G_LOW:_G_Z], ((0, 0), (0, 0), (0, LANES - GLA_RANK))).astype(BF16)
    gk_w = jnp.pad(gla_gk_w, ((0, 0), (0, LANES - GLA_RANK), (0, 0))).astype(BF16)
    w_c = w_in[:, :, _C_VAL:_M_Q].astype(BF16)
    w_mz = w_in[:, :, _M_Z:_G_Q].astype(BF16)
    w_r = w_in[:, :, _G_Z:_END].astype(BF16)
    w_branch = w_branch.astype(BF16)
    w_out = w_out.astype(BF16)

    mem_k, mem_v = _mem_kv(mem, mem_norm_g, mem_kv_w.astype(BF16))
    for l in range(DEPTH):
        h, mq, mk, mv, kmean, gq, gk, gv, gcum, glow = _seq_proj(
            l, x, norm_g, w_m, w_g, w_low, gk_w, gla_gk_b)
        y_moba = _moba(mq, mk, mv, kmean.reshape(x.shape[0], -1, BRANCH_W))
        y_gla = _gla(l, gq, gk, gv, gcum, glow, gla_norm_g)
        x = _merge(l, x, h, y_moba, y_gla, mem_k, mem_v, w_c, w_mz, w_r, conv_w, conv_b,
                   conv_ln_g, conv_ln_b, w_branch, w_out, final_norm_g)
    return x
```

```python
import functools

import jax
import jax.numpy as jnp
import numpy as np
from jax import lax
from jax.experimental import pallas as pl
from jax.experimental.pallas import tpu as pltpu

D_MODEL = 1024
DEPTH = 2
MEM_LEN = 256
N_BRANCH = 4
BRANCH_W = 512
CONV_K = 31
MOBA_HEADS = 4
MOBA_HD = 128
MOBA_BLOCK = 256
MOBA_TOPK = 3
GLA_HEADS = 4
GLA_DK = 64
GLA_DV = 128
GLA_RANK = 16
GLA_NORMALIZER = 16.0
MEM_HEADS = 4
MEM_HD = 128
EPS = 1e-6

LANES = 128
SUBLANES = 8
BF16_ROWS = 2 * SUBLANES
CONV_HALO = 32
GLA_CHUNK = 128
GLA_SAFE_LOG_DECAY = 60.0
SEQ_TILE = 1024
GLA_TILE = 512
MERGE_TILE = 512
VMEM_LIMIT = 56 * 1024 * 1024
MERGE_VMEM_LIMIT = 62 * 1024 * 1024

F32 = jnp.float32
BF16 = jnp.bfloat16
NEG_INF = float("-inf")
LOG2_E = 1.4426950408889634

_SPLITS = (512, 512, 512, 512, 512, 512, 512, 256, 256, 512, GLA_RANK, 512, 512, 512, 4096)
_OFF = tuple(int(o) for o in np.cumsum((0,) + _SPLITS))
(_C_VAL, _C_GLU, _C_Z, _M_Q, _M_K, _M_V, _M_Z, _G_Q, _G_K, _G_V, _G_LOW, _G_Z, _X_Q, _X_Z,
 _GATES, _END) = _OFF


def _rms(xf, g):
    return xf * lax.rsqrt(jnp.mean(xf * xf, axis=-1, keepdims=True) + EPS) * g


def _sigmoid(x):
    return 1.0 / (1.0 + jnp.exp(-x))


def _silu(x):
    return x * _sigmoid(x)


def _dot(a, b):
    return jnp.dot(a, b, preferred_element_type=F32)


def _dot_nt(a, b):
    return lax.dot_general(a, b, (((1,), (1,)), ((), ())), preferred_element_type=F32)


def _dot_tn(a, b):
    return lax.dot_general(a, b, (((0,), (0,)), ((), ())), preferred_element_type=F32)


def _resident(shape):
    nd = len(shape)
    return pl.BlockSpec(shape, lambda *_: (0,) * nd, pipeline_mode=pl.Buffered(1))


def _layer_resident(layer, shape):
    nd = len(shape)
    return pl.BlockSpec((None,) + tuple(shape), lambda *_: (layer,) + (0,) * nd,
                        pipeline_mode=pl.Buffered(1))


def _mem_kv_body(mem_ref, g_ref, w_ref, k_ref, v_ref):
    m = _rms(mem_ref[...], g_ref[...]).astype(BF16)
    kv = _dot(m, w_ref[...])
    k_ref[...] = kv[:, :BRANCH_W].astype(BF16)
    v_ref[...] = kv[:, BRANCH_W:].astype(BF16)


def _mem_kv(mem, mem_norm_g, kv_w):
    B = mem.shape[0]
    out = jax.ShapeDtypeStruct((DEPTH, B, MEM_LEN, BRANCH_W), BF16)
    return pl.pallas_call(
        _mem_kv_body,
        grid=(DEPTH, B),
        in_specs=[
            pl.BlockSpec((None, MEM_LEN, D_MODEL), lambda l, b: (b, 0, 0)),
            pl.BlockSpec((None, 1, D_MODEL), lambda l, b: (l, 0, 0)),
            pl.BlockSpec((None, D_MODEL, 2 * BRANCH_W), lambda l, b: (l, 0, 0)),
        ],
        out_specs=[
            pl.BlockSpec((None, None, MEM_LEN, BRANCH_W), lambda l, b: (l, b, 0, 0)),
            pl.BlockSpec((None, None, MEM_LEN, BRANCH_W), lambda l, b: (l, b, 0, 0)),
        ],
        out_shape=[out, out],
        compiler_params=pltpu.CompilerParams(
            dimension_semantics=("arbitrary", "arbitrary"), vmem_limit_bytes=VMEM_LIMIT),
        name="mem_kv",
    )(mem, mem_norm_g.reshape(DEPTH, 1, D_MODEL), kv_w)


def _seq_proj_body(x_ref, ng_ref, wm_ref, wg_ref, wlow_ref, gkw_ref, gkb_ref,
                   h_ref, mq_ref, mk_ref, mv_ref, kmean_ref, gq_ref, gk_ref, gv_ref, gcum_ref,
                   glow_ref):
    T = x_ref.shape[0]
    h = _rms(x_ref[...], ng_ref[...]).astype(BF16)
    h_ref[...] = h
    low = _dot(h, wlow_ref[...]).astype(BF16)
    z = _dot(low, gkw_ref[...]) + gkb_ref[...]
    mq = _dot(h, wm_ref[:, 0:512])
    mq_ref[...] = (mq * (MOBA_HD ** -0.5 * LOG2_E)).astype(BF16)
    log_sig = jnp.minimum(z, 0.0) - jnp.log(1.0 + jnp.exp(-jnp.abs(z)))
    g = log_sig / GLA_NORMALIZER
    mk = _dot(h, wm_ref[:, 512:1024])
    mk_ref[...] = mk.astype(BF16)
    for j in range(T // MOBA_BLOCK):
        kmean_ref[j] = jnp.mean(mk[j * MOBA_BLOCK:(j + 1) * MOBA_BLOCK], axis=0, keepdims=True)

    C = GLA_CHUNK
    ri = lax.broadcasted_iota(jnp.int32, (C, C), 0)
    ci = lax.broadcasted_iota(jnp.int32, (C, C), 1)
    tri = jnp.where(ci <= ri, 1.0, 0.0).astype(BF16)
    lowest = None
    for c in range(T // C):
        gc = g[c * C:(c + 1) * C]
        hi = gc.astype(BF16)
        rest = gc - hi.astype(F32)
        mid = rest.astype(BF16)
        lo = (rest - mid.astype(F32)).astype(BF16)
        G = _dot(tri, hi) + _dot(tri, mid) + _dot(tri, lo)
        gcum_ref[c * C:(c + 1) * C, :] = G
        lowest = G[C - 1:C, :] if lowest is None else jnp.minimum(lowest, G[C - 1:C, :])
    glow_ref[...] = lowest
    gq_ref[...] = _dot(h, wg_ref[:, 0:256])
    gk_ref[...] = _dot(h, wg_ref[:, 256:512])
    gv_ref[...] = _dot(h, wg_ref[:, 512:1024]).astype(BF16)
    mv_ref[...] = _dot(h, wm_ref[:, 1024:1536]).T.astype(BF16)


def _seq_proj(layer, x, norm_g, w_m, w_g, w_low, gk_w, gk_b):
    B, S, _ = x.shape
    T = SEQ_TILE
    nblk = T // MOBA_BLOCK
    tok = lambda w: pl.BlockSpec((None, T, w), lambda b, i: (b, i, 0))
    sd = jax.ShapeDtypeStruct
    return pl.pallas_call(
        _seq_proj_body,
        grid=(B, S // T),
        in_specs=[
            tok(D_MODEL),
            _layer_resident(layer, (1, D_MODEL)),
            _layer_resident(layer, w_m.shape[1:]),
            _layer_resident(layer, w_g.shape[1:]),
            _layer_resident(layer, w_low.shape[1:]),
            _layer_resident(layer, gk_w.shape[1:]),
            _layer_resident(layer, (1, GLA_HEADS * GLA_DK)),
        ],
        out_specs=[
            tok(D_MODEL), tok(512), tok(512),
            pl.BlockSpec((None, 512, T), lambda b, i: (b, 0, i)),
            pl.BlockSpec((None, nblk, 1, 512), lambda b, i: (b, i, 0, 0)),
            tok(256), tok(256), tok(512), tok(256),
            pl.BlockSpec((None, None, 1, 256), lambda b, i: (b, i, 0, 0)),
        ],
        out_shape=[
            sd((B, S, D_MODEL), BF16),
            sd((B, S, 512), BF16), sd((B, S, 512), BF16), sd((B, 512, S), BF16),
            sd((B, S // MOBA_BLOCK, 1, 512), F32),
            sd((B, S, 256), F32), sd((B, S, 256), F32), sd((B, S, 512), BF16),
            sd((B, S, 256), F32), sd((B, S // T, 1, 256), F32),
        ],
        compiler_params=pltpu.CompilerParams(
            dimension_semantics=("arbitrary", "arbitrary"), vmem_limit_bytes=VMEM_LIMIT),
        name="seq_proj",
    )(x, norm_g.reshape(DEPTH, 1, D_MODEL), w_m, w_g, w_low, gk_w, gk_b.reshape(DEPTH, 1, -1))


def _moba_body(q_ref, k_ref, vt_ref, kmean_ref, o_ref, vt1_ref):
    BLK = MOBA_BLOCK
    hd = MOBA_HD
    S = q_ref.shape[0]
    nb = S // BLK
    vt1_ref[0:hd, :] = vt_ref[...]
    vt1_ref[hd:, :] = jnp.ones((vt1_ref.shape[0] - hd, S), BF16)

    rs = _dot_nt(kmean_ref[...].astype(BF16), q_ref[...])
    blk = lax.broadcasted_iota(jnp.int32, (nb, S), 0)
    own = lax.broadcasted_iota(jnp.int32, (nb, S), 1) // BLK
    rs = jnp.where(blk < own, rs, NEG_INF)
    bias = jnp.full((nb, S), NEG_INF, F32)
    for _ in range(min(MOBA_TOPK, nb)):
        mx = jnp.max(rs, axis=0, keepdims=True)
        first = jnp.min(jnp.where(rs == mx, blk, nb), axis=0, keepdims=True)
        pick = (blk == first) & (mx > NEG_INF)
        bias = jnp.where(pick, 0.0, bias)
        rs = jnp.where(blk == first, NEG_INF, rs)

    key_i = lax.broadcasted_iota(jnp.int32, (BLK, BLK), 0)
    qry_i = lax.broadcasted_iota(jnp.int32, (BLK, BLK), 1)
    causal = key_i <= qry_i

    def scores(i):
        return _dot_nt(k_ref[0:(i + 1) * BLK, :], q_ref[i * BLK:(i + 1) * BLK, :])

    st_next = scores(nb - 1)
    for i in reversed(range(nb)):
        qs = slice(i * BLK, (i + 1) * BLK)
        nk = (i + 1) * BLK
        st = st_next
        if i > 0:
            st_next = scores(i - 1)
        parts = [st[n * BLK:(n + 1) * BLK] + bias[n:n + 1, qs] for n in range(i)]
        parts.append(jnp.where(causal, st[i * BLK:nk], NEG_INF))
        mx = None
        for part in parts:
            pm = jnp.max(part.reshape(BLK // SUBLANES, SUBLANES, BLK), axis=0)
            mx = pm if mx is None else jnp.maximum(mx, pm)
        m = jnp.max(mx, axis=0, keepdims=True)
        pt = jnp.concatenate([jnp.exp2(part - m).astype(BF16) for part in parts], axis=0)
        ot = _dot(vt1_ref[:, 0:nk], pt)
        o_ref[qs, :] = (ot[0:hd] / ot[hd:hd + 1]).T.astype(o_ref.dtype)


def _moba(q, k, vt, kmean):
    B, S, _ = q.shape
    nb = S // MOBA_BLOCK
    seq = pl.BlockSpec((None, S, MOBA_HD), lambda b, h: (b, 0, h))
    return pl.pallas_call(
        _moba_body,
        grid=(B, MOBA_HEADS),
        in_specs=[
            seq, seq,
            pl.BlockSpec((None, MOBA_HD, S), lambda b, h: (b, h, 0)),
            pl.BlockSpec((None, nb, MOBA_HD), lambda b, h: (b, 0, h)),
        ],
        out_specs=seq,
        out_shape=jax.ShapeDtypeStruct((B, S, BRANCH_W), BF16),
        scratch_shapes=[pltpu.VMEM((MOBA_HD + BF16_ROWS, S), BF16)],
        compiler_params=pltpu.CompilerParams(
            dimension_semantics=("arbitrary", "arbitrary"), vmem_limit_bytes=VMEM_LIMIT),
        name="moba",
    )(q, k, vt, kmean)


def _gla_body(q_ref, k_ref, v_ref, gcum_ref, glow_ref, ng_ref, o_ref, state_ref, vf_ref):
    T = q_ref.shape[0]
    C = GLA_CHUNK
    H, dk, dv = GLA_HEADS, GLA_DK, GLA_DV
    n_chunks = T // C

    @pl.when(pl.program_id(1) == 0)
    def _():
        state_ref[...] = jnp.zeros_like(state_ref)

    ri = lax.broadcasted_iota(jnp.int32, (C, C), 0)
    ci = lax.broadcasted_iota(jnp.int32, (C, C), 1)
    causal = ci <= ri
    factorable = jnp.min(glow_ref[...]) >= -GLA_SAFE_LOG_DECAY

    def run_tile(intra_fn):
        ng = ng_ref[...]
        pending = []
        for c in range(n_chunks):
            rows = pl.ds(c * C, C)
            G = gcum_ref[rows, :]
            G_last = G[C - 1:C, :]
            q = q_ref[rows, :] * (dk ** -0.5)
            k = k_ref[rows, :]
            q_in = (q * jnp.exp(G)).astype(BF16)
            k_st = (k * jnp.exp(G_last - G)).astype(BF16)
            a_last = jnp.exp(G_last)
            v = v_ref[rows, :]
            intra = intra_fn(c, q, k, G, q_in, v)
            kv = [_dot_tn(v[:, h * dv:(h + 1) * dv], k_st[:, h * dk:(h + 1) * dk])
                  for h in range(H)]
            pending.append((q_in, a_last, intra, kv))
        states = [state_ref[h] for h in range(H)]
        for c, (q_in, a_last, intra, kv) in enumerate(pending):
            rows = pl.ds(c * C, C)
            for h in range(H):
                ks = slice(h * dk, (h + 1) * dk)
                vs = slice(h * dv, (h + 1) * dv)
                o = intra[h] + _dot_nt(q_in[:, ks], states[h].astype(BF16))
                states[h] = a_last[:, ks] * states[h] + kv[h]
                o = o * lax.rsqrt(jnp.mean(o * o, axis=-1, keepdims=True) + EPS) * ng
                o_ref[rows, vs] = o.astype(o_ref.dtype)
        for h in range(H):
            state_ref[h] = states[h]

    def intra_factored(c, q, k, G, q_in, v):
        k_out = (k * jnp.exp(-G)).astype(BF16)
        out = []
        for h in range(H):
            ks = slice(h * dk, (h + 1) * dk)
            A = jnp.where(causal, _dot_nt(q_in[:, ks], k_out[:, ks]), 0.0)
            out.append(_dot(A.astype(BF16), v[:, h * dv:(h + 1) * dv]))
        return out

    def intra_keywise(c, q, k, G, q_in, v):
        d_head = lax.broadcasted_iota(jnp.int32, (H * dk, H * dv), 0) // dk
        e_head = lax.broadcasted_iota(jnp.int32, (H * dk, H * dv), 1) // dv
        head_spread = jnp.where(d_head == e_head, 1.0, 0.0).astype(F32)
        row = lax.broadcasted_iota(jnp.int32, (C, H * dk), 0)

        def add_key(j, acc):
            r = c * C + j
            k_j = k_ref[pl.ds(r, 1), :]
            G_j = gcum_ref[pl.ds(r, 1), :]
            v_j = vf_ref[pl.ds(r, 1), :]
            w = jnp.where(row >= j, q * k_j * jnp.exp(jnp.minimum(G - G_j, 0.0)), 0.0)
            a = jnp.dot(w, head_spread, preferred_element_type=F32,
                        precision=lax.Precision.HIGHEST)
            return acc + a * v_j

        o = lax.fori_loop(0, C, add_key, jnp.zeros((C, H * dv), F32))
        return [o[:, h * dv:(h + 1) * dv] for h in range(H)]

    @pl.when(factorable)
    def _():
        run_tile(intra_factored)

    @pl.when(jnp.logical_not(factorable))
    def _():
        vf_ref[...] = v_ref[...].astype(F32)
        run_tile(intra_keywise)


def _gla(layer, q, k, v, gcum, glow, norm_g):
    B, S, _ = q.shape
    T = GLA_TILE
    per_guard = SEQ_TILE // GLA_TILE
    tok = lambda w: pl.BlockSpec((None, T, w), lambda b, i: (b, i, 0))
    return pl.pallas_call(
        _gla_body,
        grid=(B, S // T),
        in_specs=[tok(256), tok(256), tok(512), tok(256),
                  pl.BlockSpec((None, None, 1, 256), lambda b, i: (b, i // per_guard, 0, 0)),
                  _layer_resident(layer, (1, GLA_DV))],
        out_specs=tok(512),
        out_shape=jax.ShapeDtypeStruct((B, S, BRANCH_W), BF16),
        scratch_shapes=[pltpu.VMEM((GLA_HEADS, GLA_DV, GLA_DK), F32),
                        pltpu.VMEM((T, GLA_HEADS * GLA_DV), F32)],
        compiler_params=pltpu.CompilerParams(
            dimension_semantics=("arbitrary", "arbitrary"), vmem_limit_bytes=VMEM_LIMIT),
        name="gla",
    )(q, k, v, gcum, glow, norm_g.reshape(-1, 1, GLA_DV))


def _merge_body(final, x_ref, h_ref, ymoba_ref, ygla_ref, memk_ref, memv_ref, wc_ref, wmz_ref,
                wr_ref, convw_ref, convb_ref, lng_ref, lnb_ref, wb_ref, wo_ref, fg_ref,
                o_ref, u_ref):
    T = x_ref.shape[0]
    W = BRANCH_W

    @pl.when(pl.program_id(1) == 0)
    def _():
        u_ref[0, 0:CONV_HALO, :] = jnp.zeros((CONV_HALO, W), F32)

    h = h_ref[...]

    def proj(w):
        return _dot(h, w)

    u_ref[0, CONV_HALO:CONV_HALO + T, :] = (
        proj(wc_ref[:, 0:W]) * _sigmoid(proj(wc_ref[:, W:2 * W])))
    span = T + CONV_HALO - SUBLANES
    for s in range(1, SUBLANES):
        u_ref[s, 0:span, :] = u_ref[0, s:s + span, :]
    R = 32
    base = CONV_HALO - (CONV_K - 1)
    conv_rows = []
    gate_logits = []
    chunks = T // R
    for r in range(chunks):
        if r % (chunks // N_BRANCH) == 0:
            g0 = 3 * W + len(gate_logits) * D_MODEL
            gate_logits.append(proj(wr_ref[:, g0:g0 + D_MODEL]))
        acc = jnp.zeros((R, W), F32) + convb_ref[...]
        for j in range(CONV_K):
            a, s = divmod(base + j, SUBLANES)
            lo = a * SUBLANES + r * R
            acc = acc + convw_ref[j:j + 1, :] * u_ref[s, lo:lo + R, :]
        conv_rows.append(acc)
    yc = jnp.concatenate(conv_rows, axis=0)
    u_ref[0, 0:CONV_HALO, :] = u_ref[0, T:T + CONV_HALO, :]
    mu = jnp.mean(yc, axis=-1, keepdims=True)
    yc = yc - mu
    var = jnp.mean(yc * yc, axis=-1, keepdims=True)
    yc = _silu(yc * lax.rsqrt(var + EPS) * lng_ref[...] + lnb_ref[...])
    y_conv = (yc * _silu(proj(wc_ref[:, 2 * W:3 * W]))).astype(BF16)

    y_moba = (ymoba_ref[...].astype(F32) * _silu(proj(wmz_ref[...]))).astype(BF16)
    y_gla = (ygla_ref[...].astype(F32) * _silu(proj(wr_ref[:, 0:W]))).astype(BF16)

    xq = (proj(wr_ref[:, W:2 * W]) * (MEM_HD ** -0.5)).astype(BF16)
    heads = []
    for hd in range(MEM_HEADS):
        ls = slice(hd * MEM_HD, (hd + 1) * MEM_HD)
        s = _dot_nt(xq[:, ls], memk_ref[:, ls])
        p = jnp.exp(s - jnp.max(s, axis=-1, keepdims=True))
        l = jnp.sum(p, axis=-1, keepdims=True)
        heads.append(_dot(p.astype(BF16), memv_ref[:, ls]) / l)
    y_mem = (jnp.concatenate(heads, axis=-1) * _silu(proj(wr_ref[:, 2 * W:3 * W]))).astype(BF16)

    merged = jnp.zeros((T, D_MODEL), F32)
    for n, y in enumerate((y_conv, y_moba, y_gla, y_mem)):
        merged = merged + _sigmoid(gate_logits[n]) * _dot(y, wb_ref[n])
    out = x_ref[...] + _dot(merged.astype(BF16), wo_ref[...])
    if final:
        out = _rms(out, fg_ref[...])
    o_ref[...] = out


def _merge(layer, x, h, y_moba, y_gla, mem_k, mem_v, w_c, w_mz, w_r, conv_w, conv_b, ln_g, ln_b,
           w_branch, w_out, final_g):
    B, S, _ = x.shape
    T = MERGE_TILE
    final = layer == DEPTH - 1
    tok = lambda w: pl.BlockSpec((None, T, w), lambda b, i: (b, i, 0))
    mem = pl.BlockSpec((None, None, MEM_LEN, BRANCH_W), lambda b, i: (layer, b, 0, 0))
    row = lambda a: a.reshape(DEPTH, 1, -1)
    return pl.pallas_call(
        functools.partial(_merge_body, final),
        grid=(B, S // T),
        in_specs=[
            tok(D_MODEL), tok(D_MODEL), tok(BRANCH_W), tok(BRANCH_W), mem, mem,
            _layer_resident(layer, w_c.shape[1:]),
            _layer_resident(layer, w_mz.shape[1:]),
            _layer_resident(layer, w_r.shape[1:]),
            _layer_resident(layer, conv_w.shape[1:]),
            _layer_resident(layer, (1, BRANCH_W)), _layer_resident(layer, (1, BRANCH_W)),
            _layer_resident(layer, (1, BRANCH_W)),
            _layer_resident(layer, w_branch.shape[1:]),
            _layer_resident(layer, w_out.shape[1:]),
            _resident((1, D_MODEL)),
        ],
        out_specs=tok(D_MODEL),
        out_shape=jax.ShapeDtypeStruct((B, S, D_MODEL), F32),
        scratch_shapes=[pltpu.VMEM((SUBLANES, T + CONV_HALO, BRANCH_W), F32)],
        compiler_params=pltpu.CompilerParams(
            dimension_semantics=("arbitrary", "arbitrary"), vmem_limit_bytes=MERGE_VMEM_LIMIT),
        name="merge_final" if final else "merge",
    )(x, h, y_moba, y_gla, mem_k, mem_v, w_c, w_mz, w_r, conv_w, row(conv_b), row(ln_g),
      row(ln_b), w_branch, w_out, final_g.reshape(1, -1))


def kernel(x, mem, norm_g, w_in, conv_w, conv_b, conv_ln_g, conv_ln_b, gla_gk_w, gla_gk_b,
           gla_norm_g, mem_norm_g, mem_kv_w, w_branch, w_out, final_norm_g):
    w_m = w_in[:, :, _M_Q:_M_Z].astype(BF16)
    w_g = w_in[:, :, _G_Q:_G_LOW].astype(BF16)
    w_low = jnp.pad(w_in[:, :, _G_LOW:_G_Z], ((0, 0), (0, 0), (0, LANES - GLA_RANK))).astype(BF16)
    gk_w = jnp.pad(gla_gk_w, ((0, 0), (0, LANES - GLA_RANK), (0, 0))).astype(BF16)
    w_c = w_in[:, :, _C_VAL:_M_Q].astype(BF16)
    w_mz = w_in[:, :, _M_Z:_G_Q].astype(BF16)
    w_r = w_in[:, :, _G_Z:_END].astype(BF16)
    w_branch = w_branch.astype(BF16)
    w_out = w_out.astype(BF16)

    mem_k, mem_v = _mem_kv(mem, mem_norm_g, mem_kv_w.astype(BF16))
    for l in range(DEPTH):
        h, mq, mk, mv, kmean, gq, gk, gv, gcum, glow = _seq_proj(
            l, x, norm_g, w_m, w_g, w_low, gk_w, gla_gk_b)
        y_moba = _moba(mq, mk, mv, kmean.reshape(x.shape[0], -1, BRANCH_W))
        y_gla = _gla(l, gq, gk, gv, gcum, glow, gla_norm_g)
        x = _merge(l, x, h, y_moba, y_gla, mem_k, mem_v, w_c, w_mz, w_r, conv_w, conv_b,
                   conv_ln_g, conv_ln_b, w_branch, w_out, final_norm_g)
    return x
```

```python
import functools

import jax
import jax.numpy as jnp
import numpy as np
from jax import lax
from jax.experimental import pallas as pl
from jax.experimental.pallas import tpu as pltpu

D_MODEL = 1024
DEPTH = 2
MEM_LEN = 256
N_BRANCH = 4
BRANCH_W = 512
CONV_K = 31
MOBA_HEADS = 4
MOBA_HD = 128
MOBA_BLOCK = 256
MOBA_TOPK = 3
GLA_HEADS = 4
GLA_DK = 64
GLA_DV = 128
GLA_RANK = 16
GLA_NORMALIZER = 16.0
MEM_HEADS = 4
MEM_HD = 128
EPS = 1e-6

LANES = 128
SUBLANES = 8
BF16_ROWS = 2 * SUBLANES
CONV_HALO = 32
GLA_CHUNK = 128
GLA_SAFE_LOG_DECAY = 60.0
SEQ_TILE = 1024
GLA_TILE = 512
MERGE_TILE = 256
W_PREP_ROWS = 128
VMEM_LIMIT = 56 * 1024 * 1024

F32 = jnp.float32
BF16 = jnp.bfloat16
NEG_INF = float("-inf")
LOG2_E = 1.4426950408889634

_SPLITS = (512, 512, 512, 512, 512, 512, 512, 256, 256, 512, GLA_RANK, 512, 512, 512, 4096)
_OFF = tuple(int(o) for o in np.cumsum((0,) + _SPLITS))
(_C_VAL, _C_GLU, _C_Z, _M_Q, _M_K, _M_V, _M_Z, _G_Q, _G_K, _G_V, _G_LOW, _G_Z, _X_Q, _X_Z,
 _GATES, _END) = _OFF


def _rms(xf, g):
    return xf * lax.rsqrt(jnp.mean(xf * xf, axis=-1, keepdims=True) + EPS) * g


def _sigmoid(x):
    return 1.0 / (1.0 + jnp.exp(-x))


def _silu(x):
    return x * _sigmoid(x)


def _dot(a, b):
    return jnp.dot(a, b, preferred_element_type=F32)


def _dot_nt(a, b):
    return lax.dot_general(a, b, (((1,), (1,)), ((), ())), preferred_element_type=F32)


def _dot_tn(a, b):
    return lax.dot_general(a, b, (((0,), (0,)), ((), ())), preferred_element_type=F32)


def _resident(shape):
    nd = len(shape)
    return pl.BlockSpec(shape, lambda *_: (0,) * nd, pipeline_mode=pl.Buffered(1))


def _layer_resident(layer, shape):
    nd = len(shape)
    return pl.BlockSpec((None,) + tuple(shape), lambda *_: (layer,) + (0,) * nd,
                        pipeline_mode=pl.Buffered(1))


def _w_prep_body(w_ref, wc_ref, wm_ref, wmz_ref, wg_ref, wlow_ref, wr_ref):
    wc_ref[...] = w_ref[:, _C_VAL:_M_Q].astype(BF16)
    wm_ref[...] = w_ref[:, _M_Q:_M_Z].astype(BF16)
    wmz_ref[...] = w_ref[:, _M_Z:_G_Q].astype(BF16)
    wg_ref[...] = w_ref[:, _G_Q:_G_LOW].astype(BF16)
    low = w_ref[:, _G_LOW:_G_LOW + LANES]
    lane = lax.broadcasted_iota(jnp.int32, low.shape, 1)
    wlow_ref[...] = jnp.where(lane < GLA_RANK, low, 0.0).astype(BF16)
    wr_ref[...] = w_ref[:, _G_Z:_END].astype(BF16)


def _w_prep(w_in):
    R = W_PREP_ROWS
    sd = jax.ShapeDtypeStruct
    widths = (_M_Q - _C_VAL, _M_Z - _M_Q, _G_Q - _M_Z, _G_LOW - _G_Q, LANES, _END - _G_Z)
    return pl.pallas_call(
        _w_prep_body,
        grid=(DEPTH, D_MODEL // R),
        in_specs=[pl.BlockSpec((None, R, _END), lambda l, r: (l, r, 0))],
        out_specs=[pl.BlockSpec((None, R, w), lambda l, r: (l, r, 0)) for w in widths],
        out_shape=[sd((DEPTH, D_MODEL, w), BF16) for w in widths],
        compiler_params=pltpu.CompilerParams(
            dimension_semantics=("arbitrary", "arbitrary"), vmem_limit_bytes=VMEM_LIMIT),
        name="w_prep",
    )(w_in)


def _mem_kv_body(mem_ref, g_ref, w_ref, k_ref, v_ref):
    m = _rms(mem_ref[...], g_ref[...]).astype(BF16)
    kv = _dot(m, w_ref[...])
    k_ref[...] = kv[:, :BRANCH_W].astype(BF16)
    v_ref[...] = kv[:, BRANCH_W:].astype(BF16)


def _mem_kv(mem, mem_norm_g, kv_w):
    B = mem.shape[0]
    out = jax.ShapeDtypeStruct((DEPTH, B, MEM_LEN, BRANCH_W), BF16)
    return pl.pallas_call(
        _mem_kv_body,
        grid=(DEPTH, B),
        in_specs=[
            pl.BlockSpec((None, MEM_LEN, D_MODEL), lambda l, b: (b, 0, 0)),
            pl.BlockSpec((None, 1, D_MODEL), lambda l, b: (l, 0, 0)),
            pl.BlockSpec((None, D_MODEL, 2 * BRANCH_W), lambda l, b: (l, 0, 0)),
        ],
        out_specs=[
            pl.BlockSpec((None, None, MEM_LEN, BRANCH_W), lambda l, b: (l, b, 0, 0)),
            pl.BlockSpec((None, None, MEM_LEN, BRANCH_W), lambda l, b: (l, b, 0, 0)),
        ],
        out_shape=[out, out],
        compiler_params=pltpu.CompilerParams(
            dimension_semantics=("arbitrary", "arbitrary"), vmem_limit_bytes=VMEM_LIMIT),
        name="mem_kv",
    )(mem, mem_norm_g.reshape(DEPTH, 1, D_MODEL), kv_w)


def _seq_proj_body(x_ref, ng_ref, wm_ref, wg_ref, wlow_ref, gkw_ref, gkb_ref,
                   h_ref, mq_ref, mk_ref, mv_ref, kmean_ref, gq_ref, gk_ref, gv_ref, gcum_ref,
                   glow_ref):
    T = x_ref.shape[0]
    h = _rms(x_ref[...], ng_ref[...]).astype(BF16)
    h_ref[...] = h
    low = _dot(h, wlow_ref[...]).astype(BF16)
    z = _dot(low, gkw_ref[...]) + gkb_ref[...]
    mq = _dot(h, wm_ref[:, 0:512])
    mq_ref[...] = (mq * (MOBA_HD ** -0.5 * LOG2_E)).astype(BF16)
    log_sig = jnp.minimum(z, 0.0) - jnp.log(1.0 + jnp.exp(-jnp.abs(z)))
    g = log_sig / GLA_NORMALIZER
    mk = _dot(h, wm_ref[:, 512:1024])
    mk_ref[...] = mk.astype(BF16)
    for j in range(T // MOBA_BLOCK):
        kmean_ref[j] = jnp.mean(mk[j * MOBA_BLOCK:(j + 1) * MOBA_BLOCK], axis=0, keepdims=True)

    C = GLA_CHUNK
    ri = lax.broadcasted_iota(jnp.int32, (C, C), 0)
    ci = lax.broadcasted_iota(jnp.int32, (C, C), 1)
    tri = jnp.where(ci <= ri, 1.0, 0.0).astype(BF16)
    lowest = None
    for c in range(T // C):
        gc = g[c * C:(c + 1) * C]
        hi = gc.astype(BF16)
        rest = gc - hi.astype(F32)
        mid = rest.astype(BF16)
        lo = (rest - mid.astype(F32)).astype(BF16)
        G = _dot(tri, hi) + _dot(tri, mid) + _dot(tri, lo)
        gcum_ref[c * C:(c + 1) * C, :] = G
        lowest = G[C - 1:C, :] if lowest is None else jnp.minimum(lowest, G[C - 1:C, :])
    glow_ref[...] = lowest
    gq_ref[...] = _dot(h, wg_ref[:, 0:256])
    gk_ref[...] = _dot(h, wg_ref[:, 256:512])
    gv_ref[...] = _dot(h, wg_ref[:, 512:1024]).astype(BF16)
    mv_ref[...] = _dot(h, wm_ref[:, 1024:1536]).T.astype(BF16)


def _seq_proj(layer, x, norm_g, w_m, w_g, w_low, gk_w, gk_b):
    B, S, _ = x.shape
    T = SEQ_TILE
    nblk = T // MOBA_BLOCK
    tok = lambda w: pl.BlockSpec((None, T, w), lambda b, i: (b, i, 0))
    sd = jax.ShapeDtypeStruct
    return pl.pallas_call(
        _seq_proj_body,
        grid=(B, S // T),
        in_specs=[
            tok(D_MODEL),
            _layer_resident(layer, (1, D_MODEL)),
            _layer_resident(layer, w_m.shape[1:]),
            _layer_resident(layer, w_g.shape[1:]),
            _layer_resident(layer, w_low.shape[1:]),
            _layer_resident(layer, gk_w.shape[1:]),
            _layer_resident(layer, (1, GLA_HEADS * GLA_DK)),
        ],
        out_specs=[
            tok(D_MODEL), tok(512), tok(512),
            pl.BlockSpec((None, 512, T), lambda b, i: (b, 0, i)),
            pl.BlockSpec((None, nblk, 1, 512), lambda b, i: (b, i, 0, 0)),
            tok(256), tok(256), tok(512), tok(256),
            pl.BlockSpec((None, None, 1, 256), lambda b, i: (b, i, 0, 0)),
        ],
        out_shape=[
            sd((B, S, D_MODEL), BF16),
            sd((B, S, 512), BF16), sd((B, S, 512), BF16), sd((B, 512, S), BF16),
            sd((B, S // MOBA_BLOCK, 1, 512), F32),
            sd((B, S, 256), F32), sd((B, S, 256), F32), sd((B, S, 512), BF16),
            sd((B, S, 256), F32), sd((B, S // T, 1, 256), F32),
        ],
        compiler_params=pltpu.CompilerParams(
            dimension_semantics=("arbitrary", "arbitrary"), vmem_limit_bytes=VMEM_LIMIT),
        name="seq_proj",
    )(x, norm_g.reshape(DEPTH, 1, D_MODEL), w_m, w_g, w_low, gk_w, gk_b.reshape(DEPTH, 1, -1))


def _moba_body(q_ref, k_ref, vt_ref, kmean_ref, o_ref, vt1_ref):
    BLK = MOBA_BLOCK
    hd = MOBA_HD
    S = q_ref.shape[0]
    nb = S // BLK
    vt1_ref[0:hd, :] = vt_ref[...]
    vt1_ref[hd:, :] = jnp.ones((vt1_ref.shape[0] - hd, S), BF16)

    rs = _dot_nt(kmean_ref[...].astype(BF16), q_ref[...])
    blk = lax.broadcasted_iota(jnp.int32, (nb, S), 0)
    own = lax.broadcasted_iota(jnp.int32, (nb, S), 1) // BLK
    rs = jnp.where(blk < own, rs, NEG_INF)
    bias = jnp.full((nb, S), NEG_INF, F32)
    for _ in range(min(MOBA_TOPK, nb)):
        mx = jnp.max(rs, axis=0, keepdims=True)
        first = jnp.min(jnp.where(rs == mx, blk, nb), axis=0, keepdims=True)
        pick = (blk == first) & (mx > NEG_INF)
        bias = jnp.where(pick, 0.0, bias)
        rs = jnp.where(blk == first, NEG_INF, rs)

    key_i = lax.broadcasted_iota(jnp.int32, (BLK, BLK), 0)
    qry_i = lax.broadcasted_iota(jnp.int32, (BLK, BLK), 1)
    causal = key_i <= qry_i

    def scores(i):
        return _dot_nt(k_ref[0:(i + 1) * BLK, :], q_ref[i * BLK:(i + 1) * BLK, :])

    st_next = scores(nb - 1)
    for i in reversed(range(nb)):
        qs = slice(i * BLK, (i + 1) * BLK)
        nk = (i + 1) * BLK
        st = st_next
        if i > 0:
            st_next = scores(i - 1)
        parts = [st[n * BLK:(n + 1) * BLK] + bias[n:n + 1, qs] for n in range(i)]
        parts.append(jnp.where(causal, st[i * BLK:nk], NEG_INF))
        mx = None
        for part in parts:
            pm = jnp.max(part.reshape(BLK // SUBLANES, SUBLANES, BLK), axis=0)
            mx = pm if mx is None else jnp.maximum(mx, pm)
        m = jnp.max(mx, axis=0, keepdims=True)
        pt = jnp.concatenate([jnp.exp2(part - m).astype(BF16) for part in parts], axis=0)
        ot = _dot(vt1_ref[:, 0:nk], pt)
        o_ref[qs, :] = (ot[0:hd] / ot[hd:hd + 1]).T.astype(o_ref.dtype)


def _moba(q, k, vt, kmean):
    B, S, _ = q.shape
    nb = S // MOBA_BLOCK
    seq = pl.BlockSpec((None, S, MOBA_HD), lambda b, h: (b, 0, h))
    return pl.pallas_call(
        _moba_body,
        grid=(B, MOBA_HEADS),
        in_specs=[
            seq, seq,
            pl.BlockSpec((None, MOBA_HD, S), lambda b, h: (b, h, 0)),
            pl.BlockSpec((None, nb, MOBA_HD), lambda b, h: (b, 0, h)),
        ],
        out_specs=seq,
        out_shape=jax.ShapeDtypeStruct((B, S, BRANCH_W), BF16),
        scratch_shapes=[pltpu.VMEM((MOBA_HD + BF16_ROWS, S), BF16)],
        compiler_params=pltpu.CompilerParams(
            dimension_semantics=("arbitrary", "arbitrary"), vmem_limit_bytes=VMEM_LIMIT),
        name="moba",
    )(q, k, vt, kmean)


def _gla_body(q_ref, k_ref, v_ref, gcum_ref, glow_ref, ng_ref, o_ref, state_ref, vf_ref):
    T = q_ref.shape[0]
    C = GLA_CHUNK
    H, dk, dv = GLA_HEADS, GLA_DK, GLA_DV
    n_chunks = T // C

    @pl.when(pl.program_id(1) == 0)
    def _():
        state_ref[...] = jnp.zeros_like(state_ref)

    ri = lax.broadcasted_iota(jnp.int32, (C, C), 0)
    ci = lax.broadcasted_iota(jnp.int32, (C, C), 1)
    causal = ci <= ri
    factorable = jnp.min(glow_ref[...]) >= -GLA_SAFE_LOG_DECAY

    def run_tile(intra_fn):
        ng = ng_ref[...]
        pending = []
        for c in range(n_chunks):
            rows = pl.ds(c * C, C)
            G = gcum_ref[rows, :]
            G_last = G[C - 1:C, :]
            q = q_ref[rows, :] * (dk ** -0.5)
            k = k_ref[rows, :]
            q_in = (q * jnp.exp(G)).astype(BF16)
            k_st = (k * jnp.exp(G_last - G)).astype(BF16)
            a_last = jnp.exp(G_last)
            v = v_ref[rows, :]
            intra = intra_fn(c, q, k, G, q_in, v)
            kv = [_dot_tn(v[:, h * dv:(h + 1) * dv], k_st[:, h * dk:(h + 1) * dk])
                  for h in range(H)]
            pending.append((q_in, a_last, intra, kv))
        states = [state_ref[h] for h in range(H)]
        for c, (q_in, a_last, intra, kv) in enumerate(pending):
            rows = pl.ds(c * C, C)
            for h in range(H):
                ks = slice(h * dk, (h + 1) * dk)
                vs = slice(h * dv, (h + 1) * dv)
                o = intra[h] + _dot_nt(q_in[:, ks], states[h].astype(BF16))
                states[h] = a_last[:, ks] * states[h] + kv[h]
                o = o * lax.rsqrt(jnp.mean(o * o, axis=-1, keepdims=True) + EPS) * ng
                o_ref[rows, vs] = o.astype(o_ref.dtype)
        for h in range(H):
            state_ref[h] = states[h]

    def intra_factored(c, q, k, G, q_in, v):
        k_out = (k * jnp.exp(-G)).astype(BF16)
        out = []
        for h in range(H):
            ks = slice(h * dk, (h + 1) * dk)
            A = jnp.where(causal, _dot_nt(q_in[:, ks], k_out[:, ks]), 0.0)
            out.append(_dot(A.astype(BF16), v[:, h * dv:(h + 1) * dv]))
        return out

    def intra_keywise(c, q, k, G, q_in, v):
        d_head = lax.broadcasted_iota(jnp.int32, (H * dk, H * dv), 0) // dk
        e_head = lax.broadcasted_iota(jnp.int32, (H * dk, H * dv), 1) // dv
        head_spread = jnp.where(d_head == e_head, 1.0, 0.0).astype(F32)
        row = lax.broadcasted_iota(jnp.int32, (C, H * dk), 0)

        def add_key(j, acc):
            r = c * C + j
            k_j = k_ref[pl.ds(r, 1), :]
            G_j = gcum_ref[pl.ds(r, 1), :]
            v_j = vf_ref[pl.ds(r, 1), :]
            w = jnp.where(row >= j, q * k_j * jnp.exp(jnp.minimum(G - G_j, 0.0)), 0.0)
            a = jnp.dot(w, head_spread, preferred_element_type=F32,
                        precision=lax.Precision.HIGHEST)
            return acc + a * v_j

        o = lax.fori_loop(0, C, add_key, jnp.zeros((C, H * dv), F32))
        return [o[:, h * dv:(h + 1) * dv] for h in range(H)]

    @pl.when(factorable)
    def _():
        run_tile(intra_factored)

    @pl.when(jnp.logical_not(factorable))
    def _():
        vf_ref[...] = v_ref[...].astype(F32)
        run_tile(intra_keywise)


def _gla(layer, q, k, v, gcum, glow, norm_g):
    B, S, _ = q.shape
    T = GLA_TILE
    per_guard = SEQ_TILE // GLA_TILE
    tok = lambda w: pl.BlockSpec((None, T, w), lambda b, i: (b, i, 0))
    return pl.pallas_call(
        _gla_body,
        grid=(B, S // T),
        in_specs=[tok(256), tok(256), tok(512), tok(256),
                  pl.BlockSpec((None, None, 1, 256), lambda b, i: (b, i // per_guard, 0, 0)),
                  _layer_resident(layer, (1, GLA_DV))],
        out_specs=tok(512),
        out_shape=jax.ShapeDtypeStruct((B, S, BRANCH_W), BF16),
        scratch_shapes=[pltpu.VMEM((GLA_HEADS, GLA_DV, GLA_DK), F32),
                        pltpu.VMEM((T, GLA_HEADS * GLA_DV), F32)],
        compiler_params=pltpu.CompilerParams(
            dimension_semantics=("arbitrary", "arbitrary"), vmem_limit_bytes=VMEM_LIMIT),
        name="gla",
    )(q, k, v, gcum, glow, norm_g.reshape(-1, 1, GLA_DV))


def _merge_body(final, x_ref, h_ref, ymoba_ref, ygla_ref, memk_ref, memv_ref, wc_ref, wmz_ref,
                wr_ref, convw_ref, convb_ref, lng_ref, lnb_ref, wb_ref, wo_ref, fg_ref,
                o_ref, u_ref):
    T = x_ref.shape[0]
    W = BRANCH_W

    @pl.when(pl.program_id(1) == 0)
    def _():
        u_ref[0, 0:CONV_HALO, :] = jnp.zeros((CONV_HALO, W), F32)

    h = h_ref[...]

    def proj(w):
        return _dot(h, w)

    u_ref[0, CONV_HALO:CONV_HALO + T, :] = (
        proj(wc_ref[:, 0:W]) * _sigmoid(proj(wc_ref[:, W:2 * W])))
    span = T + CONV_HALO - SUBLANES
    for s in range(1, SUBLANES):
        u_ref[s, 0:span, :] = u_ref[0, s:s + span, :]
    R = 32
    base = CONV_HALO - (CONV_K - 1)
    conv_rows = []
    gate_logits = []
    chunks = T // R
    for r in range(chunks):
        if r % (chunks // N_BRANCH) == 0:
            g0 = 3 * W + len(gate_logits) * D_MODEL
            gate_logits.append(proj(wr_ref[:, g0:g0 + D_MODEL]))
        acc = jnp.zeros((R, W), F32) + convb_ref[...]
        for j in range(CONV_K):
            a, s = divmod(base + j, SUBLANES)
            lo = a * SUBLANES + r * R
            acc = acc + convw_ref[j:j + 1, :] * u_ref[s, lo:lo + R, :]
        conv_rows.append(acc)
    yc = jnp.concatenate(conv_rows, axis=0)
    u_ref[0, 0:CONV_HALO, :] = u_ref[0, T:T + CONV_HALO, :]
    mu = jnp.mean(yc, axis=-1, keepdims=True)
    yc = yc - mu
    var = jnp.mean(yc * yc, axis=-1, keepdims=True)
    yc = _silu(yc * lax.rsqrt(var + EPS) * lng_ref[...] + lnb_ref[...])
    y_conv = (yc * _silu(proj(wc_ref[:, 2 * W:3 * W]))).astype(BF16)

    y_moba = (ymoba_ref[...].astype(F32) * _silu(proj(wmz_ref[...]))).astype(BF16)
    y_gla = (ygla_ref[...].astype(F32) * _silu(proj(wr_ref[:, 0:W]))).astype(BF16)

    xq = (proj(wr_ref[:, W:2 * W]) * (MEM_HD ** -0.5)).astype(BF16)
    heads = []
    for hd in range(MEM_HEADS):
        ls = slice(hd * MEM_HD, (hd + 1) * MEM_HD)
        s = _dot_nt(xq[:, ls], memk_ref[:, ls])
        p = jnp.exp(s - jnp.max(s, axis=-1, keepdims=True))
        l = jnp.sum(p, axis=-1, keepdims=True)
        heads.append(_dot(p.astype(BF16), memv_ref[:, ls]) / l)
    y_mem = (jnp.concatenate(heads, axis=-1) * _silu(proj(wr_ref[:, 2 * W:3 * W]))).astype(BF16)

    merged = jnp.zeros((T, D_MODEL), F32)
    for n, y in enumerate((y_conv, y_moba, y_gla, y_mem)):
        merged = merged + _sigmoid(gate_logits[n]) * _dot(y, wb_ref[n])
    out = x_ref[...] + _dot(merged.astype(BF16), wo_ref[...])
    if final:
        out = _rms(out, fg_ref[...])
    o_ref[...] = out


def _merge(layer, x, h, y_moba, y_gla, mem_k, mem_v, w_c, w_mz, w_r, conv_w, conv_b, ln_g, ln_b,
           w_branch, w_out, final_g):
    B, S, _ = x.shape
    T = MERGE_TILE
    final = layer == DEPTH - 1
    tok = lambda w: pl.BlockSpec((None, T, w), lambda b, i: (b, i, 0))
    mem = pl.BlockSpec((None, None, MEM_LEN, BRANCH_W), lambda b, i: (layer, b, 0, 0))
    row = lambda a: a.reshape(DEPTH, 1, -1)
    return pl.pallas_call(
        functools.partial(_merge_body, final),
        grid=(B, S // T),
        in_specs=[
            tok(D_MODEL), tok(D_MODEL), tok(BRANCH_W), tok(BRANCH_W), mem, mem,
            _layer_resident(layer, w_c.shape[1:]),
            _layer_resident(layer, w_mz.shape[1:]),
            _layer_resident(layer, w_r.shape[1:]),
            _layer_resident(layer, conv_w.shape[1:]),
            _layer_resident(layer, (1, BRANCH_W)), _layer_resident(layer, (1, BRANCH_W)),
            _layer_resident(layer, (1, BRANCH_W)),
            _layer_resident(layer, w_branch.shape[1:]),
            _layer_resident(layer, w_out.shape[1:]),
            _resident((1, D_MODEL)),
        ],
        out_specs=tok(D_MODEL),
        out_shape=jax.ShapeDtypeStruct((B, S, D_MODEL), F32),
        scratch_shapes=[pltpu.VMEM((SUBLANES, T + CONV_HALO, BRANCH_W), F32)],
        compiler_params=pltpu.CompilerParams(
            dimension_semantics=("arbitrary", "arbitrary"), vmem_limit_bytes=VMEM_LIMIT),
        name="merge_final" if final else "merge",
    )(x, h, y_moba, y_gla, mem_k, mem_v, w_c, w_mz, w_r, conv_w, row(conv_b), row(ln_g),
      row(ln_b), w_branch, w_out, final_g.reshape(1, -1))


def kernel(x, mem, norm_g, w_in, conv_w, conv_b, conv_ln_g, conv_ln_b, gla_gk_w, gla_gk_b,
           gla_norm_g, mem_norm_g, mem_kv_w, w_branch, w_out, final_norm_g):
    w_c, w_m, w_mz, w_g, w_low, w_r = _w_prep(w_in)
    gk_w = jnp.pad(gla_gk_w, ((0, 0), (0, LANES - GLA_RANK), (0, 0))).astype(BF16)
    w_branch = w_branch.astype(BF16)
    w_out = w_out.astype(BF16)

    mem_k, mem_v = _mem_kv(mem, mem_norm_g, mem_kv_w.astype(BF16))
    for l in range(DEPTH):
        h, mq, mk, mv, kmean, gq, gk, gv, gcum, glow = _seq_proj(
            l, x, norm_g, w_m, w_g, w_low, gk_w, gla_gk_b)
        y_moba = _moba(mq, mk, mv, kmean.reshape(x.shape[0], -1, BRANCH_W))
        y_gla = _gla(l, gq, gk, gv, gcum, glow, gla_norm_g)
        x = _merge(l, x, h, y_moba, y_gla, mem_k, mem_v, w_c, w_mz, w_r, conv_w, conv_b,
                   conv_ln_g, conv_ln_b, w_branch, w_out, final_norm_g)
    return x
```

```python
import functools

import jax
import jax.numpy as jnp
import numpy as np
from jax import lax
from jax.experimental import pallas as pl
from jax.experimental.pallas import tpu as pltpu

D_MODEL = 1024
DEPTH = 2
MEM_LEN = 256
N_BRANCH = 4
BRANCH_W = 512
CONV_K = 31
MOBA_HEADS = 4
MOBA_HD = 128
MOBA_BLOCK = 256
MOBA_TOPK = 3
GLA_HEADS = 4
GLA_DK = 64
GLA_DV = 128
GLA_RANK = 16
GLA_NORMALIZER = 16.0
MEM_HEADS = 4
MEM_HD = 128
EPS = 1e-6

LANES = 128
SUBLANES = 8
BF16_ROWS = 2 * SUBLANES
CONV_HALO = 32
GLA_CHUNK = 128
GLA_SAFE_LOG_DECAY = 60.0
SEQ_TILE = 1024
GLA_TILE = 512
MERGE_TILE = 256
VMEM_LIMIT = 56 * 1024 * 1024

F32 = jnp.float32
BF16 = jnp.bfloat16
NEG_INF = float("-inf")
LOG2_E = 1.4426950408889634

_SPLITS = (512, 512, 512, 512, 512, 512, 512, 256, 256, 512, GLA_RANK, 512, 512, 512, 4096)
_OFF = tuple(int(o) for o in np.cumsum((0,) + _SPLITS))
(_C_VAL, _C_GLU, _C_Z, _M_Q, _M_K, _M_V, _M_Z, _G_Q, _G_K, _G_V, _G_LOW, _G_Z, _X_Q, _X_Z,
 _GATES, _END) = _OFF


def _rms(xf, g):
    return xf * lax.rsqrt(jnp.mean(xf * xf, axis=-1, keepdims=True) + EPS) * g


def _sigmoid(x):
    return 1.0 / (1.0 + jnp.exp(-x))


def _silu(x):
    return x * _sigmoid(x)


def _dot(a, b):
    return jnp.dot(a, b, preferred_element_type=F32)


def _dot_nt(a, b):
    return lax.dot_general(a, b, (((1,), (1,)), ((), ())), preferred_element_type=F32)


def _dot_tn(a, b):
    return lax.dot_general(a, b, (((0,), (0,)), ((), ())), preferred_element_type=F32)


def _resident(shape):
    nd = len(shape)
    return pl.BlockSpec(shape, lambda *_: (0,) * nd, pipeline_mode=pl.Buffered(1))


def _layer_resident(layer, shape):
    nd = len(shape)
    return pl.BlockSpec((None,) + tuple(shape), lambda *_: (layer,) + (0,) * nd,
                        pipeline_mode=pl.Buffered(1))


def _mem_kv_body(mem_ref, g_ref, w_ref, k_ref, v_ref):
    m = _rms(mem_ref[...], g_ref[...]).astype(BF16)
    kv = _dot(m, w_ref[...])
    k_ref[...] = kv[:, :BRANCH_W].astype(BF16)
    v_ref[...] = kv[:, BRANCH_W:].astype(BF16)


def _mem_kv(mem, mem_norm_g, kv_w):
    B = mem.shape[0]
    out = jax.ShapeDtypeStruct((DEPTH, B, MEM_LEN, BRANCH_W), BF16)
    return pl.pallas_call(
        _mem_kv_body,
        grid=(DEPTH, B),
        in_specs=[
            pl.BlockSpec((None, MEM_LEN, D_MODEL), lambda l, b: (b, 0, 0)),
            pl.BlockSpec((None, 1, D_MODEL), lambda l, b: (l, 0, 0)),
            pl.BlockSpec((None, D_MODEL, 2 * BRANCH_W), lambda l, b: (l, 0, 0)),
        ],
        out_specs=[
            pl.BlockSpec((None, None, MEM_LEN, BRANCH_W), lambda l, b: (l, b, 0, 0)),
            pl.BlockSpec((None, None, MEM_LEN, BRANCH_W), lambda l, b: (l, b, 0, 0)),
        ],
        out_shape=[out, out],
        compiler_params=pltpu.CompilerParams(
            dimension_semantics=("arbitrary", "arbitrary"), vmem_limit_bytes=VMEM_LIMIT),
        name="mem_kv",
    )(mem, mem_norm_g.reshape(DEPTH, 1, D_MODEL), kv_w)


def _seq_proj_body(x_ref, ng_ref, wm_ref, wg_ref, wlow_ref, gkw_ref, gkb_ref,
                   h_ref, mq_ref, mk_ref, mv_ref, kmean_ref, gq_ref, gk_ref, gv_ref, gcum_ref,
                   glow_ref):
    T = x_ref.shape[0]
    half = T // 2
    q_scale = MOBA_HD ** -0.5 * LOG2_E
    h_top = _rms(x_ref[0:half, :], ng_ref[...]).astype(BF16)
    h_ref[0:half, :] = h_top
    mq_ref[0:half, :] = (_dot(h_top, wm_ref[:, 0:512]) * q_scale).astype(BF16)
    h_bot = _rms(x_ref[half:T, :], ng_ref[...]).astype(BF16)
    h_ref[half:T, :] = h_bot
    mq_ref[half:T, :] = (_dot(h_bot, wm_ref[:, 0:512]) * q_scale).astype(BF16)
    h = jnp.concatenate([h_top, h_bot], axis=0)
    low = _dot(h, wlow_ref[...]).astype(BF16)
    mk = _dot(h, wm_ref[:, 512:1024])
    mk_ref[...] = mk.astype(BF16)
    for j in range(T // MOBA_BLOCK):
        kmean_ref[j] = jnp.mean(mk[j * MOBA_BLOCK:(j + 1) * MOBA_BLOCK], axis=0, keepdims=True)
    z = _dot(low, gkw_ref[...]) + gkb_ref[...]
    mv_ref[...] = _dot(h, wm_ref[:, 1024:1536]).T.astype(BF16)
    log_sig = jnp.minimum(z, 0.0) - jnp.log(1.0 + jnp.exp(-jnp.abs(z)))
    g = log_sig / GLA_NORMALIZER
    gq_ref[...] = _dot(h, wg_ref[:, 0:256])
    gk_ref[...] = _dot(h, wg_ref[:, 256:512])

    C = GLA_CHUNK
    ri = lax.broadcasted_iota(jnp.int32, (C, C), 0)
    ci = lax.broadcasted_iota(jnp.int32, (C, C), 1)
    tri = jnp.where(ci <= ri, 1.0, 0.0).astype(BF16)
    lowest = None
    for c in range(T // C):
        gc = g[c * C:(c + 1) * C]
        hi = gc.astype(BF16)
        rest = gc - hi.astype(F32)
        mid = rest.astype(BF16)
        lo = (rest - mid.astype(F32)).astype(BF16)
        G = _dot(tri, hi) + _dot(tri, mid) + _dot(tri, lo)
        gcum_ref[c * C:(c + 1) * C, :] = G
        lowest = G[C - 1:C, :] if lowest is None else jnp.minimum(lowest, G[C - 1:C, :])
    glow_ref[...] = lowest
    gv_ref[...] = _dot(h, wg_ref[:, 512:1024]).astype(BF16)


def _seq_proj(layer, x, norm_g, w_m, w_g, w_low, gk_w, gk_b):
    B, S, _ = x.shape
    T = SEQ_TILE
    nblk = T // MOBA_BLOCK
    tok = lambda w: pl.BlockSpec((None, T, w), lambda b, i: (b, i, 0))
    sd = jax.ShapeDtypeStruct
    return pl.pallas_call(
        _seq_proj_body,
        grid=(B, S // T),
        in_specs=[
            tok(D_MODEL),
            _layer_resident(layer, (1, D_MODEL)),
            _layer_resident(layer, w_m.shape[1:]),
            _layer_resident(layer, w_g.shape[1:]),
            _layer_resident(layer, w_low.shape[1:]),
            _layer_resident(layer, gk_w.shape[1:]),
            _layer_resident(layer, (1, GLA_HEADS * GLA_DK)),
        ],
        out_specs=[
            tok(D_MODEL), tok(512), tok(512),
            pl.BlockSpec((None, 512, T), lambda b, i: (b, 0, i)),
            pl.BlockSpec((None, nblk, 1, 512), lambda b, i: (b, i, 0, 0)),
            tok(256), tok(256), tok(512), tok(256),
            pl.BlockSpec((None, None, 1, 256), lambda b, i: (b, i, 0, 0)),
        ],
        out_shape=[
            sd((B, S, D_MODEL), BF16),
            sd((B, S, 512), BF16), sd((B, S, 512), BF16), sd((B, 512, S), BF16),
            sd((B, S // MOBA_BLOCK, 1, 512), F32),
            sd((B, S, 256), F32), sd((B, S, 256), F32), sd((B, S, 512), BF16),
            sd((B, S, 256), F32), sd((B, S // T, 1, 256), F32),
        ],
        compiler_params=pltpu.CompilerParams(
            dimension_semantics=("arbitrary", "arbitrary"), vmem_limit_bytes=VMEM_LIMIT),
        name="seq_proj",
    )(x, norm_g.reshape(DEPTH, 1, D_MODEL), w_m, w_g, w_low, gk_w, gk_b.reshape(DEPTH, 1, -1))


def _moba_body(q_ref, k_ref, vt_ref, kmean_ref, o_ref, vt1_ref):
    BLK = MOBA_BLOCK
    hd = MOBA_HD
    S = q_ref.shape[0]
    nb = S // BLK
    vt1_ref[0:hd, :] = vt_ref[...]
    vt1_ref[hd:, :] = jnp.ones((vt1_ref.shape[0] - hd, S), BF16)

    rs = _dot_nt(kmean_ref[...].astype(BF16), q_ref[...])
    blk = lax.broadcasted_iota(jnp.int32, (nb, S), 0)
    own = lax.broadcasted_iota(jnp.int32, (nb, S), 1) // BLK
    rs = jnp.where(blk < own, rs, NEG_INF)
    bias = jnp.full((nb, S), NEG_INF, F32)
    for _ in range(min(MOBA_TOPK, nb)):
        mx = jnp.max(rs, axis=0, keepdims=True)
        first = jnp.min(jnp.where(rs == mx, blk, nb), axis=0, keepdims=True)
        pick = (blk == first) & (mx > NEG_INF)
        bias = jnp.where(pick, 0.0, bias)
        rs = jnp.where(blk == first, NEG_INF, rs)

    key_i = lax.broadcasted_iota(jnp.int32, (BLK, BLK), 0)
    qry_i = lax.broadcasted_iota(jnp.int32, (BLK, BLK), 1)
    causal = key_i <= qry_i

    def scores(i):
        return _dot_nt(k_ref[0:(i + 1) * BLK, :], q_ref[i * BLK:(i + 1) * BLK, :])

    st_next = scores(nb - 1)
    for i in reversed(range(nb)):
        qs = slice(i * BLK, (i + 1) * BLK)
        nk = (i + 1) * BLK
        st = st_next
        if i > 0:
            st_next = scores(i - 1)
        parts = [st[n * BLK:(n + 1) * BLK] + bias[n:n + 1, qs] for n in range(i)]
        parts.append(jnp.where(causal, st[i * BLK:nk], NEG_INF))
        mx = None
        for part in parts:
            pm = jnp.max(part.reshape(BLK // SUBLANES, SUBLANES, BLK), axis=0)
            mx = pm if mx is None else jnp.maximum(mx, pm)
        m = jnp.max(mx, axis=0, keepdims=True)
        pt = jnp.concatenate([jnp.exp2(part - m).astype(BF16) for part in parts], axis=0)
        ot = _dot(vt1_ref[:, 0:nk], pt)
        o_ref[qs, :] = (ot[0:hd] / ot[hd:hd + 1]).T.astype(o_ref.dtype)


def _moba(q, k, vt, kmean):
    B, S, _ = q.shape
    nb = S // MOBA_BLOCK
    seq = pl.BlockSpec((None, S, MOBA_HD), lambda b, h: (b, 0, h))
    return pl.pallas_call(
        _moba_body,
        grid=(B, MOBA_HEADS),
        in_specs=[
            seq, seq,
            pl.BlockSpec((None, MOBA_HD, S), lambda b, h: (b, h, 0)),
            pl.BlockSpec((None, nb, MOBA_HD), lambda b, h: (b, 0, h)),
        ],
        out_specs=seq,
        out_shape=jax.ShapeDtypeStruct((B, S, BRANCH_W), BF16),
        scratch_shapes=[pltpu.VMEM((MOBA_HD + BF16_ROWS, S), BF16)],
        compiler_params=pltpu.CompilerParams(
            dimension_semantics=("arbitrary", "arbitrary"), vmem_limit_bytes=VMEM_LIMIT),
        name="moba",
    )(q, k, vt, kmean)


def _gla_body(q_ref, k_ref, v_ref, gcum_ref, glow_ref, ng_ref, o_ref, state_ref, vf_ref):
    T = q_ref.shape[0]
    C = GLA_CHUNK
    H, dk, dv = GLA_HEADS, GLA_DK, GLA_DV
    n_chunks = T // C

    @pl.when(pl.program_id(1) == 0)
    def _():
        state_ref[...] = jnp.zeros_like(state_ref)

    ri = lax.broadcasted_iota(jnp.int32, (C, C), 0)
    ci = lax.broadcasted_iota(jnp.int32, (C, C), 1)
    causal = ci <= ri
    factorable = jnp.min(glow_ref[...]) >= -GLA_SAFE_LOG_DECAY

    def run_tile(intra_fn):
        ng = ng_ref[...]
        pending = []
        for c in range(n_chunks):
            rows = pl.ds(c * C, C)
            G = gcum_ref[rows, :]
            G_last = G[C - 1:C, :]
            q = q_ref[rows, :] * (dk ** -0.5)
            k = k_ref[rows, :]
            q_in = (q * jnp.exp(G)).astype(BF16)
            k_st = (k * jnp.exp(G_last - G)).astype(BF16)
            a_last = jnp.exp(G_last)
            v = v_ref[rows, :]
            intra = intra_fn(c, q, k, G, q_in, v)
            kv = [_dot_tn(v[:, h * dv:(h + 1) * dv], k_st[:, h * dk:(h + 1) * dk])
                  for h in range(H)]
            pending.append((q_in, a_last, intra, kv))
        states = [state_ref[h] for h in range(H)]
        for c, (q_in, a_last, intra, kv) in enumerate(pending):
            rows = pl.ds(c * C, C)
            for h in range(H):
                ks = slice(h * dk, (h + 1) * dk)
                vs = slice(h * dv, (h + 1) * dv)
                o = intra[h] + _dot_nt(q_in[:, ks], states[h].astype(BF16))
                states[h] = a_last[:, ks] * states[h] + kv[h]
                o = o * lax.rsqrt(jnp.mean(o * o, axis=-1, keepdims=True) + EPS) * ng
                o_ref[rows, vs] = o.astype(o_ref.dtype)
        for h in range(H):
            state_ref[h] = states[h]

    def intra_factored(c, q, k, G, q_in, v):
        k_out = (k * jnp.exp(-G)).astype(BF16)
        out = []
        for h in range(H):
            ks = slice(h * dk, (h + 1) * dk)
            A = jnp.where(causal, _dot_nt(q_in[:, ks], k_out[:, ks]), 0.0)
            out.append(_dot(A.astype(BF16), v[:, h * dv:(h + 1) * dv]))
        return out

    def intra_keywise(c, q, k, G, q_in, v):
        d_head = lax.broadcasted_iota(jnp.int32, (H * dk, H * dv), 0) // dk
        e_head = lax.broadcasted_iota(jnp.int32, (H * dk, H * dv), 1) // dv
        head_spread = jnp.where(d_head == e_head, 1.0, 0.0).astype(F32)
        row = lax.broadcasted_iota(jnp.int32, (C, H * dk), 0)

        def add_key(j, acc):
            r = c * C + j
            k_j = k_ref[pl.ds(r, 1), :]
            G_j = gcum_ref[pl.ds(r, 1), :]
            v_j = vf_ref[pl.ds(r, 1), :]
            w = jnp.where(row >= j, q * k_j * jnp.exp(jnp.minimum(G - G_j, 0.0)), 0.0)
            a = jnp.dot(w, head_spread, preferred_element_type=F32,
                        precision=lax.Precision.HIGHEST)
            return acc + a * v_j

        o = lax.fori_loop(0, C, add_key, jnp.zeros((C, H * dv), F32))
        return [o[:, h * dv:(h + 1) * dv] for h in range(H)]

    @pl.when(factorable)
    def _():
        run_tile(intra_factored)

    @pl.when(jnp.logical_not(factorable))
    def _():
        vf_ref[...] = v_ref[...].astype(F32)
        run_tile(intra_keywise)


def _gla(layer, q, k, v, gcum, glow, norm_g):
    B, S, _ = q.shape
    T = GLA_TILE
    per_guard = SEQ_TILE // GLA_TILE
    tok = lambda w: pl.BlockSpec((None, T, w), lambda b, i: (b, i, 0))
    return pl.pallas_call(
        _gla_body,
        grid=(B, S // T),
        in_specs=[tok(256), tok(256), tok(512), tok(256),
                  pl.BlockSpec((None, None, 1, 256), lambda b, i: (b, i // per_guard, 0, 0)),
                  _layer_resident(layer, (1, GLA_DV))],
        out_specs=tok(512),
        out_shape=jax.ShapeDtypeStruct((B, S, BRANCH_W), BF16),
        scratch_shapes=[pltpu.VMEM((GLA_HEADS, GLA_DV, GLA_DK), F32),
                        pltpu.VMEM((T, GLA_HEADS * GLA_DV), F32)],
        compiler_params=pltpu.CompilerParams(
            dimension_semantics=("arbitrary", "arbitrary"), vmem_limit_bytes=VMEM_LIMIT),
        name="gla",
    )(q, k, v, gcum, glow, norm_g.reshape(-1, 1, GLA_DV))


def _merge_body(final, x_ref, h_ref, ymoba_ref, ygla_ref, memk_ref, memv_ref, wc_ref, wmz_ref,
                wr_ref, convw_ref, convb_ref, lng_ref, lnb_ref, wb_ref, wo_ref, fg_ref,
                o_ref, u_ref):
    T = x_ref.shape[0]
    W = BRANCH_W

    @pl.when(pl.program_id(1) == 0)
    def _():
        u_ref[0, 0:CONV_HALO, :] = jnp.zeros((CONV_HALO, W), F32)

    h = h_ref[...]

    def proj(w):
        return _dot(h, w)

    u_ref[0, CONV_HALO:CONV_HALO + T, :] = (
        proj(wc_ref[:, 0:W]) * _sigmoid(proj(wc_ref[:, W:2 * W])))
    span = T + CONV_HALO - SUBLANES
    for s in range(1, SUBLANES):
        u_ref[s, 0:span, :] = u_ref[0, s:s + span, :]
    R = 32
    base = CONV_HALO - (CONV_K - 1)
    conv_rows = []
    chunks = T // R
    gate_cols = [wr_ref.at[:, 3 * W + n * D_MODEL:3 * W + (n + 1) * D_MODEL]
                 for n in range(N_BRANCH)]
    early_cols = gate_cols + [wmz_ref, wr_ref.at[:, 0:W], wr_ref.at[:, W:2 * W],
                              wr_ref.at[:, 2 * W:3 * W]]
    assert len(early_cols) <= chunks
    early = []
    for r in range(chunks):
        if r < len(early_cols):
            early.append(proj(early_cols[r][...]))
        acc = jnp.zeros((R, W), F32) + convb_ref[...]
        for j in range(CONV_K):
            a, s = divmod(base + j, SUBLANES)
            lo = a * SUBLANES + r * R
            acc = acc + convw_ref[j:j + 1, :] * u_ref[s, lo:lo + R, :]
        conv_rows.append(acc)
    yc = jnp.concatenate(conv_rows, axis=0)
    u_ref[0, 0:CONV_HALO, :] = u_ref[0, T:T + CONV_HALO, :]
    gate_logits = early[:N_BRANCH]
    moba_z, gla_z, mem_q, mem_z = early[N_BRANCH:]
    y_moba = (ymoba_ref[...].astype(F32) * _silu(moba_z)).astype(BF16)
    y_gla = (ygla_ref[...].astype(F32) * _silu(gla_z)).astype(BF16)

    xq = (mem_q * (MEM_HD ** -0.5)).astype(BF16)
    heads = []
    for hd in range(MEM_HEADS):
        ls = slice(hd * MEM_HD, (hd + 1) * MEM_HD)
        s = _dot_nt(xq[:, ls], memk_ref[:, ls])
        p = jnp.exp(s - jnp.max(s, axis=-1, keepdims=True))
        l = jnp.sum(p, axis=-1, keepdims=True)
        heads.append(_dot(p.astype(BF16), memv_ref[:, ls]) / l)
    y_mem = (jnp.concatenate(heads, axis=-1) * _silu(mem_z)).astype(BF16)

    mu = jnp.mean(yc, axis=-1, keepdims=True)
    yc = yc - mu
    var = jnp.mean(yc * yc, axis=-1, keepdims=True)
    yc = _silu(yc * lax.rsqrt(var + EPS) * lng_ref[...] + lnb_ref[...])
    y_conv = (yc * _silu(proj(wc_ref[:, 2 * W:3 * W]))).astype(BF16)

    merged = jnp.zeros((T, D_MODEL), F32)
    for n, y in ((1, y_moba), (2, y_gla), (3, y_mem), (0, y_conv)):
        merged = merged + _sigmoid(gate_logits[n]) * _dot(y, wb_ref[n])
    out = x_ref[...] + _dot(merged.astype(BF16), wo_ref[...])
    if final:
        out = _rms(out, fg_ref[...])
    o_ref[...] = out


def _merge(layer, x, h, y_moba, y_gla, mem_k, mem_v, w_c, w_mz, w_r, conv_w, conv_b, ln_g, ln_b,
           w_branch, w_out, final_g):
    B, S, _ = x.shape
    T = MERGE_TILE
    final = layer == DEPTH - 1
    tok = lambda w: pl.BlockSpec((None, T, w), lambda b, i: (b, i, 0))
    mem = pl.BlockSpec((None, None, MEM_LEN, BRANCH_W), lambda b, i: (layer, b, 0, 0))
    row = lambda a: a.reshape(DEPTH, 1, -1)
    return pl.pallas_call(
        functools.partial(_merge_body, final),
        grid=(B, S // T),
        in_specs=[
            tok(D_MODEL), tok(D_MODEL), tok(BRANCH_W), tok(BRANCH_W), mem, mem,
            _layer_resident(layer, w_c.shape[1:]),
            _layer_resident(layer, w_mz.shape[1:]),
            _layer_resident(layer, w_r.shape[1:]),
            _layer_resident(layer, conv_w.shape[1:]),
            _layer_resident(layer, (1, BRANCH_W)), _layer_resident(layer, (1, BRANCH_W)),
            _layer_resident(layer, (1, BRANCH_W)),
            _layer_resident(layer, w_branch.shape[1:]),
            _layer_resident(layer, w_out.shape[1:]),
            _resident((1, D_MODEL)),
        ],
        out_specs=tok(D_MODEL),
        out_shape=jax.ShapeDtypeStruct((B, S, D_MODEL), F32),
        scratch_shapes=[pltpu.VMEM((SUBLANES, T + CONV_HALO, BRANCH_W), F32)],
        compiler_params=pltpu.CompilerParams(
            dimension_semantics=("arbitrary", "arbitrary"), vmem_limit_bytes=VMEM_LIMIT),
        name="merge_final" if final else "merge",
    )(x, h, y_moba, y_gla, mem_k, mem_v, w_c, w_mz, w_r, conv_w, row(conv_b), row(ln_g),
      row(ln_b), w_branch, w_out, final_g.reshape(1, -1))


def kernel(x, mem, norm_g, w_in, conv_w, conv_b, conv_ln_g, conv_ln_b, gla_gk_w, gla_gk_b,
           gla_norm_g, mem_norm_g, mem_kv_w, w_branch, w_out, final_norm_g):
    w_m = w_in[:, :, _M_Q:_M_Z].astype(BF16)
    w_g = w_in[:, :, _G_Q:_G_LOW].astype(BF16)
    w_low = jnp.pad(w_in[:, :, _G_LOW:_G_Z], ((0, 0), (0, 0), (0, LANES - GLA_RANK))).astype(BF16)
    gk_w = jnp.pad(gla_gk_w, ((0, 0), (0, LANES - GLA_RANK), (0, 0))).astype(BF16)
    w_c = w_in[:, :, _C_VAL:_M_Q].astype(BF16)
    w_mz = w_in[:, :, _M_Z:_G_Q].astype(BF16)
    w_r = w_in[:, :, _G_Z:_END].astype(BF16)
    w_branch = w_branch.astype(BF16)
    w_out = w_out.astype(BF16)

    mem_k, mem_v = _mem_kv(mem, mem_norm_g, mem_kv_w.astype(BF16))
    for l in range(DEPTH):
        h, mq, mk, mv, kmean, gq, gk, gv, gcum, glow = _seq_proj(
            l, x, norm_g, w_m, w_g, w_low, gk_w, gla_gk_b)
        y_moba = _moba(mq, mk, mv, kmean.reshape(x.shape[0], -1, BRANCH_W))
        y_gla = _gla(l, gq, gk, gv, gcum, glow, gla_norm_g)
        x = _merge(l, x, h, y_moba, y_gla, mem_k, mem_v, w_c, w_mz, w_r, conv_w, conv_b,
                   conv_ln_g, conv_ln_b, w_branch, w_out, final_norm_g)
    return x
```

```python
import functools

import jax
import jax.numpy as jnp
import numpy as np
from jax import lax
from jax.experimental import pallas as pl
from jax.experimental.pallas import tpu as pltpu

D_MODEL = 1024
DEPTH = 2
MEM_LEN = 256
N_BRANCH = 4
BRANCH_W = 512
CONV_K = 31
MOBA_HEADS = 4
MOBA_HD = 128
MOBA_BLOCK = 256
MOBA_TOPK = 3
GLA_HEADS = 4
GLA_DK = 64
GLA_DV = 128
GLA_RANK = 16
GLA_NORMALIZER = 16.0
MEM_HEADS = 4
MEM_HD = 128
EPS = 1e-6

LANES = 128
SUBLANES = 8
BF16_ROWS = 2 * SUBLANES
CONV_HALO = 32
GLA_CHUNK = 128
GLA_SAFE_LOG_DECAY = 60.0
SEQ_TILE = 1024
GLA_TILE = 512
MERGE_TILE = 256
VMEM_LIMIT = 56 * 1024 * 1024

F32 = jnp.float32
BF16 = jnp.bfloat16
NEG_INF = float("-inf")
LOG2_E = 1.4426950408889634

_SPLITS = (512, 512, 512, 512, 512, 512, 512, 256, 256, 512, GLA_RANK, 512, 512, 512, 4096)
_OFF = tuple(int(o) for o in np.cumsum((0,) + _SPLITS))
(_C_VAL, _C_GLU, _C_Z, _M_Q, _M_K, _M_V, _M_Z, _G_Q, _G_K, _G_V, _G_LOW, _G_Z, _X_Q, _X_Z,
 _GATES, _END) = _OFF


def _rms(xf, g):
    return xf * lax.rsqrt(jnp.mean(xf * xf, axis=-1, keepdims=True) + EPS) * g


def _sigmoid(x):
    return 1.0 / (1.0 + jnp.exp(-x))


def _silu(x):
    return x * _sigmoid(x)


def _dot(a, b):
    return jnp.dot(a, b, preferred_element_type=F32)


def _dot_nt(a, b):
    return lax.dot_general(a, b, (((1,), (1,)), ((), ())), preferred_element_type=F32)


def _dot_tn(a, b):
    return lax.dot_general(a, b, (((0,), (0,)), ((), ())), preferred_element_type=F32)


def _resident(shape):
    nd = len(shape)
    return pl.BlockSpec(shape, lambda *_: (0,) * nd, pipeline_mode=pl.Buffered(1))


def _layer_resident(layer, shape):
    nd = len(shape)
    return pl.BlockSpec((None,) + tuple(shape), lambda *_: (layer,) + (0,) * nd,
                        pipeline_mode=pl.Buffered(1))


def _mem_kv_body(mem_ref, g_ref, w_ref, k_ref, v_ref):
    m = _rms(mem_ref[...], g_ref[...]).astype(BF16)
    kv = _dot(m, w_ref[...])
    k_ref[...] = kv[:, :BRANCH_W].astype(BF16)
    v_ref[...] = kv[:, BRANCH_W:].astype(BF16)


def _mem_kv(mem, mem_norm_g, kv_w):
    B = mem.shape[0]
    out = jax.ShapeDtypeStruct((DEPTH, B, MEM_LEN, BRANCH_W), BF16)
    return pl.pallas_call(
        _mem_kv_body,
        grid=(DEPTH, B),
        in_specs=[
            pl.BlockSpec((None, MEM_LEN, D_MODEL), lambda l, b: (b, 0, 0)),
            pl.BlockSpec((None, 1, D_MODEL), lambda l, b: (l, 0, 0)),
            pl.BlockSpec((None, D_MODEL, 2 * BRANCH_W), lambda l, b: (l, 0, 0)),
        ],
        out_specs=[
            pl.BlockSpec((None, None, MEM_LEN, BRANCH_W), lambda l, b: (l, b, 0, 0)),
            pl.BlockSpec((None, None, MEM_LEN, BRANCH_W), lambda l, b: (l, b, 0, 0)),
        ],
        out_shape=[out, out],
        compiler_params=pltpu.CompilerParams(
            dimension_semantics=("arbitrary", "arbitrary"), vmem_limit_bytes=VMEM_LIMIT),
        name="mem_kv",
    )(mem, mem_norm_g.reshape(DEPTH, 1, D_MODEL), kv_w)


def _seq_proj_body(x_ref, ng_ref, wm_ref, wg_ref, wlow_ref, gkw_ref, gkb_ref,
                   h_ref, mq_ref, mk_ref, mv_ref, kmean_ref, gq_ref, gk_ref, gv_ref, gcum_ref,
                   glow_ref):
    T = x_ref.shape[0]
    half = T // 2
    q_scale = MOBA_HD ** -0.5 * LOG2_E
    h_top = _rms(x_ref[0:half, :], ng_ref[...]).astype(BF16)
    h_ref[0:half, :] = h_top
    mq_ref[0:half, :] = (_dot(h_top, wm_ref[:, 0:512]) * q_scale).astype(BF16)
    h_bot = _rms(x_ref[half:T, :], ng_ref[...]).astype(BF16)
    h_ref[half:T, :] = h_bot
    mq_ref[half:T, :] = (_dot(h_bot, wm_ref[:, 0:512]) * q_scale).astype(BF16)
    h = jnp.concatenate([h_top, h_bot], axis=0)
    low = _dot(h, wlow_ref[...]).astype(BF16)
    mk = _dot(h, wm_ref[:, 512:1024])
    mk_ref[...] = mk.astype(BF16)
    for j in range(T // MOBA_BLOCK):
        kmean_ref[j] = jnp.mean(mk[j * MOBA_BLOCK:(j + 1) * MOBA_BLOCK], axis=0, keepdims=True)
    z = _dot(low, gkw_ref[...]) + gkb_ref[...]
    mv_ref[...] = _dot(h, wm_ref[:, 1024:1536]).T.astype(BF16)
    log_sig = jnp.minimum(z, 0.0) - jnp.log(1.0 + jnp.exp(-jnp.abs(z)))
    g = log_sig / GLA_NORMALIZER
    gq_ref[...] = _dot(h, wg_ref[:, 0:256])
    gk_ref[...] = _dot(h, wg_ref[:, 256:512])

    C = GLA_CHUNK
    ri = lax.broadcasted_iota(jnp.int32, (C, C), 0)
    ci = lax.broadcasted_iota(jnp.int32, (C, C), 1)
    tri = jnp.where(ci <= ri, 1.0, 0.0).astype(BF16)
    lowest = None
    for c in range(T // C):
        gc = g[c * C:(c + 1) * C]
        hi = gc.astype(BF16)
        rest = gc - hi.astype(F32)
        mid = rest.astype(BF16)
        lo = (rest - mid.astype(F32)).astype(BF16)
        G = _dot(tri, hi) + _dot(tri, mid) + _dot(tri, lo)
        gcum_ref[c * C:(c + 1) * C, :] = G
        lowest = G[C - 1:C, :] if lowest is None else jnp.minimum(lowest, G[C - 1:C, :])
    glow_ref[...] = lowest
    gv_ref[...] = _dot(h, wg_ref[:, 512:1024]).astype(BF16)


def _seq_proj(layer, x, norm_g, w_m, w_g, w_low, gk_w, gk_b):
    B, S, _ = x.shape
    T = SEQ_TILE
    nblk = T // MOBA_BLOCK
    tok = lambda w: pl.BlockSpec((None, T, w), lambda b, i: (b, i, 0))
    sd = jax.ShapeDtypeStruct
    return pl.pallas_call(
        _seq_proj_body,
        grid=(B, S // T),
        in_specs=[
            tok(D_MODEL),
            _layer_resident(layer, (1, D_MODEL)),
            _layer_resident(layer, w_m.shape[1:]),
            _layer_resident(layer, w_g.shape[1:]),
            _layer_resident(layer, w_low.shape[1:]),
            _layer_resident(layer, gk_w.shape[1:]),
            _layer_resident(layer, (1, GLA_HEADS * GLA_DK)),
        ],
        out_specs=[
            tok(D_MODEL), tok(512), tok(512),
            pl.BlockSpec((None, 512, T), lambda b, i: (b, 0, i)),
            pl.BlockSpec((None, nblk, 1, 512), lambda b, i: (b, i, 0, 0)),
            tok(256), tok(256), tok(512), tok(256),
            pl.BlockSpec((None, None, 1, 256), lambda b, i: (b, i, 0, 0)),
        ],
        out_shape=[
            sd((B, S, D_MODEL), BF16),
            sd((B, S, 512), BF16), sd((B, S, 512), BF16), sd((B, 512, S), BF16),
            sd((B, S // MOBA_BLOCK, 1, 512), F32),
            sd((B, S, 256), F32), sd((B, S, 256), F32), sd((B, S, 512), BF16),
            sd((B, S, 256), F32), sd((B, S // T, 1, 256), F32),
        ],
        compiler_params=pltpu.CompilerParams(
            dimension_semantics=("arbitrary", "arbitrary"), vmem_limit_bytes=VMEM_LIMIT),
        name="seq_proj",
    )(x, norm_g.reshape(DEPTH, 1, D_MODEL), w_m, w_g, w_low, gk_w, gk_b.reshape(DEPTH, 1, -1))


def _moba_body(q_ref, k_ref, vt_ref, kmean_ref, o_ref, vt1_ref):
    BLK = MOBA_BLOCK
    hd = MOBA_HD
    S = q_ref.shape[0]
    nb = S // BLK

    def scores(i):
        return _dot_nt(k_ref[0:(i + 1) * BLK, :], q_ref[i * BLK:(i + 1) * BLK, :])

    st_next = scores(nb - 1)
    vt1_ref[0:hd, :] = vt_ref[...]
    vt1_ref[hd:, :] = jnp.ones((vt1_ref.shape[0] - hd, S), BF16)

    rs = _dot_nt(kmean_ref[...].astype(BF16), q_ref[...])
    blk = lax.broadcasted_iota(jnp.int32, (nb, S), 0)
    own = lax.broadcasted_iota(jnp.int32, (nb, S), 1) // BLK
    rs = jnp.where(blk < own, rs, NEG_INF)
    bias = jnp.full((nb, S), NEG_INF, F32)
    for _ in range(min(MOBA_TOPK, nb)):
        mx = jnp.max(rs, axis=0, keepdims=True)
        first = jnp.min(jnp.where(rs == mx, blk, nb), axis=0, keepdims=True)
        pick = (blk == first) & (mx > NEG_INF)
        bias = jnp.where(pick, 0.0, bias)
        rs = jnp.where(blk == first, NEG_INF, rs)

    key_i = lax.broadcasted_iota(jnp.int32, (BLK, BLK), 0)
    qry_i = lax.broadcasted_iota(jnp.int32, (BLK, BLK), 1)
    causal = key_i <= qry_i

    for i in reversed(range(nb)):
        qs = slice(i * BLK, (i + 1) * BLK)
        nk = (i + 1) * BLK
        st = st_next
        if i > 0:
            st_next = scores(i - 1)
        parts = [st[n * BLK:(n + 1) * BLK] + bias[n:n + 1, qs] for n in range(i)]
        parts.append(jnp.where(causal, st[i * BLK:nk], NEG_INF))
        mx = None
        for part in parts:
            pm = jnp.max(part.reshape(BLK // SUBLANES, SUBLANES, BLK), axis=0)
            mx = pm if mx is None else jnp.maximum(mx, pm)
        m = jnp.max(mx, axis=0, keepdims=True)
        pt = jnp.concatenate([jnp.exp2(part - m).astype(BF16) for part in parts], axis=0)
        ot = _dot(vt1_ref[:, 0:nk], pt)
        o_ref[qs, :] = (ot[0:hd] / ot[hd:hd + 1]).T.astype(o_ref.dtype)


def _moba(q, k, vt, kmean):
    B, S, _ = q.shape
    nb = S // MOBA_BLOCK
    seq = pl.BlockSpec((None, S, MOBA_HD), lambda b, h: (b, 0, h))
    return pl.pallas_call(
        _moba_body,
        grid=(B, MOBA_HEADS),
        in_specs=[
            seq, seq,
            pl.BlockSpec((None, MOBA_HD, S), lambda b, h: (b, h, 0)),
            pl.BlockSpec((None, nb, MOBA_HD), lambda b, h: (b, 0, h)),
        ],
        out_specs=seq,
        out_shape=jax.ShapeDtypeStruct((B, S, BRANCH_W), BF16),
        scratch_shapes=[pltpu.VMEM((MOBA_HD + BF16_ROWS, S), BF16)],
        compiler_params=pltpu.CompilerParams(
            dimension_semantics=("arbitrary", "arbitrary"), vmem_limit_bytes=VMEM_LIMIT),
        name="moba",
    )(q, k, vt, kmean)


def _gla_body(q_ref, k_ref, v_ref, gcum_ref, glow_ref, ng_ref, o_ref, state_ref, vf_ref):
    T = q_ref.shape[0]
    C = GLA_CHUNK
    H, dk, dv = GLA_HEADS, GLA_DK, GLA_DV
    n_chunks = T // C

    @pl.when(pl.program_id(1) == 0)
    def _():
        state_ref[...] = jnp.zeros_like(state_ref)

    ri = lax.broadcasted_iota(jnp.int32, (C, C), 0)
    ci = lax.broadcasted_iota(jnp.int32, (C, C), 1)
    causal = ci <= ri
    factorable = jnp.min(glow_ref[...]) >= -GLA_SAFE_LOG_DECAY

    def run_tile(intra_fn):
        ng = ng_ref[...]
        pending = []
        for c in range(n_chunks):
            rows = pl.ds(c * C, C)
            G = gcum_ref[rows, :]
            G_last = G[C - 1:C, :]
            q = q_ref[rows, :] * (dk ** -0.5)
            k = k_ref[rows, :]
            q_in = (q * jnp.exp(G)).astype(BF16)
            k_st = (k * jnp.exp(G_last - G)).astype(BF16)
            a_last = jnp.exp(G_last)
            v = v_ref[rows, :]
            intra = intra_fn(c, q, k, G, q_in, v)
            kv = [_dot_tn(v[:, h * dv:(h + 1) * dv], k_st[:, h * dk:(h + 1) * dk])
                  for h in range(H)]
            pending.append((q_in, a_last, intra, kv))
        states = [state_ref[h] for h in range(H)]
        entry = []
        for q_in, a_last, intra, kv in pending:
            entry.append(states)
            states = [a_last[:, h * dk:(h + 1) * dk] * states[h] + kv[h] for h in range(H)]
        for h in range(H):
            state_ref[h] = states[h]
        for c, (q_in, a_last, intra, kv) in enumerate(pending):
            rows = pl.ds(c * C, C)
            for h in range(H):
                ks = slice(h * dk, (h + 1) * dk)
                vs = slice(h * dv, (h + 1) * dv)
                o = intra[h] + _dot_nt(q_in[:, ks], entry[c][h].astype(BF16))
                o = o * lax.rsqrt(jnp.mean(o * o, axis=-1, keepdims=True) + EPS) * ng
                o_ref[rows, vs] = o.astype(o_ref.dtype)

    def intra_factored(c, q, k, G, q_in, v):
        k_out = (k * jnp.exp(-G)).astype(BF16)
        out = []
        for h in range(H):
            ks = slice(h * dk, (h + 1) * dk)
            A = jnp.where(causal, _dot_nt(q_in[:, ks], k_out[:, ks]), 0.0)
            out.append(_dot(A.astype(BF16), v[:, h * dv:(h + 1) * dv]))
        return out

    def intra_keywise(c, q, k, G, q_in, v):
        d_head = lax.broadcasted_iota(jnp.int32, (H * dk, H * dv), 0) // dk
        e_head = lax.broadcasted_iota(jnp.int32, (H * dk, H * dv), 1) // dv
        head_spread = jnp.where(d_head == e_head, 1.0, 0.0).astype(F32)
        row = lax.broadcasted_iota(jnp.int32, (C, H * dk), 0)

        def add_key(j, acc):
            r = c * C + j
            k_j = k_ref[pl.ds(r, 1), :]
            G_j = gcum_ref[pl.ds(r, 1), :]
            v_j = vf_ref[pl.ds(r, 1), :]
            w = jnp.where(row >= j, q * k_j * jnp.exp(jnp.minimum(G - G_j, 0.0)), 0.0)
            a = jnp.dot(w, head_spread, preferred_element_type=F32,
                        precision=lax.Precision.HIGHEST)
            return acc + a * v_j

        o = lax.fori_loop(0, C, add_key, jnp.zeros((C, H * dv), F32))
        return [o[:, h * dv:(h + 1) * dv] for h in range(H)]

    @pl.when(factorable)
    def _():
        run_tile(intra_factored)

    @pl.when(jnp.logical_not(factorable))
    def _():
        vf_ref[...] = v_ref[...].astype(F32)
        run_tile(intra_keywise)


def _gla(layer, q, k, v, gcum, glow, norm_g):
    B, S, _ = q.shape
    T = GLA_TILE
    per_guard = SEQ_TILE // GLA_TILE
    tok = lambda w: pl.BlockSpec((None, T, w), lambda b, i: (b, i, 0))
    return pl.pallas_call(
        _gla_body,
        grid=(B, S // T),
        in_specs=[tok(256), tok(256), tok(512), tok(256),
                  pl.BlockSpec((None, None, 1, 256), lambda b, i: (b, i // per_guard, 0, 0)),
                  _layer_resident(layer, (1, GLA_DV))],
        out_specs=tok(512),
        out_shape=jax.ShapeDtypeStruct((B, S, BRANCH_W), BF16),
        scratch_shapes=[pltpu.VMEM((GLA_HEADS, GLA_DV, GLA_DK), F32),
                        pltpu.VMEM((T, GLA_HEADS * GLA_DV), F32)],
        compiler_params=pltpu.CompilerParams(
            dimension_semantics=("arbitrary", "arbitrary"), vmem_limit_bytes=VMEM_LIMIT),
        name="gla",
    )(q, k, v, gcum, glow, norm_g.reshape(-1, 1, GLA_DV))


def _merge_body(final, x_ref, h_ref, ymoba_ref, ygla_ref, memk_ref, memv_ref, wc_ref, wmz_ref,
                wr_ref, convw_ref, convb_ref, lng_ref, lnb_ref, wb_ref, wo_ref, fg_ref,
                o_ref, u_ref):
    T = x_ref.shape[0]
    W = BRANCH_W

    @pl.when(pl.program_id(1) == 0)
    def _():
        u_ref[0, 0:CONV_HALO, :] = jnp.zeros((CONV_HALO, W), F32)

    h = h_ref[...]

    def proj(w):
        return _dot(h, w)

    u_ref[0, CONV_HALO:CONV_HALO + T, :] = (
        proj(wc_ref[:, 0:W]) * _sigmoid(proj(wc_ref[:, W:2 * W])))
    span = T + CONV_HALO - SUBLANES
    for s in range(1, SUBLANES):
        u_ref[s, 0:span, :] = u_ref[0, s:s + span, :]
    R = 32
    base = CONV_HALO - (CONV_K - 1)
    conv_rows = []
    chunks = T // R
    gate_cols = [wr_ref.at[:, 3 * W + n * D_MODEL:3 * W + (n + 1) * D_MODEL]
                 for n in range(N_BRANCH)]
    early_cols = gate_cols + [wmz_ref, wr_ref.at[:, 0:W], wr_ref.at[:, W:2 * W],
                              wr_ref.at[:, 2 * W:3 * W]]
    assert len(early_cols) <= chunks
    early = []
    for r in range(chunks):
        if r < len(early_cols):
            early.append(proj(early_cols[r][...]))
        acc = jnp.zeros((R, W), F32) + convb_ref[...]
        for j in range(CONV_K):
            a, s = divmod(base + j, SUBLANES)
            lo = a * SUBLANES + r * R
            acc = acc + convw_ref[j:j + 1, :] * u_ref[s, lo:lo + R, :]
        conv_rows.append(acc)
    yc = jnp.concatenate(conv_rows, axis=0)
    u_ref[0, 0:CONV_HALO, :] = u_ref[0, T:T + CONV_HALO, :]
    gate_logits = early[:N_BRANCH]
    moba_z, gla_z, mem_q, mem_z = early[N_BRANCH:]
    y_moba = (ymoba_ref[...].astype(F32) * _silu(moba_z)).astype(BF16)
    y_gla = (ygla_ref[...].astype(F32) * _silu(gla_z)).astype(BF16)

    xq = (mem_q * (MEM_HD ** -0.5)).astype(BF16)
    heads = []
    for hd in range(MEM_HEADS):
        ls = slice(hd * MEM_HD, (hd + 1) * MEM_HD)
        s = _dot_nt(xq[:, ls], memk_ref[:, ls])
        p = jnp.exp(s - jnp.max(s, axis=-1, keepdims=True))
        l = jnp.sum(p, axis=-1, keepdims=True)
        heads.append(_dot(p.astype(BF16), memv_ref[:, ls]) / l)
    y_mem = (jnp.concatenate(heads, axis=-1) * _silu(mem_z)).astype(BF16)

    mu = jnp.mean(yc, axis=-1, keepdims=True)
    yc = yc - mu
    var = jnp.mean(yc * yc, axis=-1, keepdims=True)
    yc = _silu(yc * lax.rsqrt(var + EPS) * lng_ref[...] + lnb_ref[...])
    y_conv = (yc * _silu(proj(wc_ref[:, 2 * W:3 * W]))).astype(BF16)

    merged = jnp.zeros((T, D_MODEL), F32)
    for n, y in ((1, y_moba), (2, y_gla), (3, y_mem), (0, y_conv)):
        merged = merged + _sigmoid(gate_logits[n]) * _dot(y, wb_ref[n])
    out = x_ref[...] + _dot(merged.astype(BF16), wo_ref[...])
    if final:
        out = _rms(out, fg_ref[...])
    o_ref[...] = out


def _merge(layer, x, h, y_moba, y_gla, mem_k, mem_v, w_c, w_mz, w_r, conv_w, conv_b, ln_g, ln_b,
           w_branch, w_out, final_g):
    B, S, _ = x.shape
    T = MERGE_TILE
    final = layer == DEPTH - 1
    tok = lambda w: pl.BlockSpec((None, T, w), lambda b, i: (b, i, 0))
    mem = pl.BlockSpec((None, None, MEM_LEN, BRANCH_W), lambda b, i: (layer, b, 0, 0))
    row = lambda a: a.reshape(DEPTH, 1, -1)
    return pl.pallas_call(
        functools.partial(_merge_body, final),
        grid=(B, S // T),
        in_specs=[
            tok(D_MODEL), tok(D_MODEL), tok(BRANCH_W), tok(BRANCH_W), mem, mem,
            _layer_resident(layer, w_c.shape[1:]),
            _layer_resident(layer, w_mz.shape[1:]),
            _layer_resident(layer, w_r.shape[1:]),
            _layer_resident(layer, conv_w.shape[1:]),
            _layer_resident(layer, (1, BRANCH_W)), _layer_resident(layer, (1, BRANCH_W)),
            _layer_resident(layer, (1, BRANCH_W)),
            _layer_resident(layer, w_branch.shape[1:]),
            _layer_resident(layer, w_out.shape[1:]),
            _resident((1, D_MODEL)),
        ],
        out_specs=tok(D_MODEL),
        out_shape=jax.ShapeDtypeStruct((B, S, D_MODEL), F32),
        scratch_shapes=[pltpu.VMEM((SUBLANES, T + CONV_HALO, BRANCH_W), F32)],
        compiler_params=pltpu.CompilerParams(
            dimension_semantics=("arbitrary", "arbitrary"), vmem_limit_bytes=VMEM_LIMIT),
        name="merge_final" if final else "merge",
    )(x, h, y_moba, y_gla, mem_k, mem_v, w_c, w_mz, w_r, conv_w, row(conv_b), row(ln_g),
      row(ln_b), w_branch, w_out, final_g.reshape(1, -1))


def kernel(x, mem, norm_g, w_in, conv_w, conv_b, conv_ln_g, conv_ln_b, gla_gk_w, gla_gk_b,
           gla_norm_g, mem_norm_g, mem_kv_w, w_branch, w_out, final_norm_g):
    w_m = w_in[:, :, _M_Q:_M_Z].astype(BF16)
    w_g = w_in[:, :, _G_Q:_G_LOW].astype(BF16)
    w_low = jnp.pad(w_in[:, :, _G_LOW:_G_Z], ((0, 0), (0, 0), (0, LANES - GLA_RANK))).astype(BF16)
    gk_w = jnp.pad(gla_gk_w, ((0, 0), (0, LANES - GLA_RANK), (0, 0))).astype(BF16)
    w_c = w_in[:, :, _C_VAL:_M_Q].astype(BF16)
    w_mz = w_in[:, :, _M_Z:_G_Q].astype(BF16)
    w_r = w_in[:, :, _G_Z:_END].astype(BF16)
    w_branch = w_branch.astype(BF16)
    w_out = w_out.astype(BF16)

    mem_k, mem_v = _mem_kv(mem, mem_norm_g, mem_kv_w.astype(BF16))
    for l in range(DEPTH):
        h, mq, mk, mv, kmean, gq, gk, gv, gcum, glow = _seq_proj(
            l, x, norm_g, w_m, w_g, w_low, gk_w, gla_gk_b)
        y_moba = _moba(mq, mk, mv, kmean.reshape(x.shape[0], -1, BRANCH_W))
        y_gla = _gla(l, gq, gk, gv, gcum, glow, gla_norm_g)
        x = _merge(l, x, h, y_moba, y_gla, mem_k, mem_v, w_c, w_mz, w_r, conv_w, conv_b,
                   conv_ln_g, conv_ln_b, w_branch, w_out, final_norm_g)
    return x
```

```python
import functools

import jax
import jax.numpy as jnp
import numpy as np
from jax import lax
from jax.experimental import pallas as pl
from jax.experimental.pallas import tpu as pltpu

D_MODEL = 1024
DEPTH = 2
MEM_LEN = 256
N_BRANCH = 4
BRANCH_W = 512
CONV_K = 31
MOBA_HEADS = 4
MOBA_HD = 128
MOBA_BLOCK = 256
MOBA_TOPK = 3
GLA_HEADS = 4
GLA_DK = 64
GLA_DV = 128
GLA_RANK = 16
GLA_NORMALIZER = 16.0
MEM_HEADS = 4
MEM_HD = 128
EPS = 1e-6

LANES = 128
SUBLANES = 8
BF16_ROWS = 2 * SUBLANES
CONV_HALO = 32
GLA_CHUNK = 128
GLA_SAFE_LOG_DECAY = 60.0
SEQ_TILE = 1024
GLA_TILE = 512
MERGE_TILE = 256
VMEM_LIMIT = 56 * 1024 * 1024

F32 = jnp.float32
BF16 = jnp.bfloat16
NEG_INF = float("-inf")
LOG2_E = 1.4426950408889634

_SPLITS = (512, 512, 512, 512, 512, 512, 512, 256, 256, 512, GLA_RANK, 512, 512, 512, 4096)
_OFF = tuple(int(o) for o in np.cumsum((0,) + _SPLITS))
(_C_VAL, _C_GLU, _C_Z, _M_Q, _M_K, _M_V, _M_Z, _G_Q, _G_K, _G_V, _G_LOW, _G_Z, _X_Q, _X_Z,
 _GATES, _END) = _OFF


def _rms(xf, g):
    return xf * lax.rsqrt(jnp.mean(xf * xf, axis=-1, keepdims=True) + EPS) * g


def _sigmoid(x):
    return 1.0 / (1.0 + jnp.exp(-x))


def _silu(x):
    return x * _sigmoid(x)


def _dot(a, b):
    return jnp.dot(a, b, preferred_element_type=F32)


def _dot_nt(a, b):
    return lax.dot_general(a, b, (((1,), (1,)), ((), ())), preferred_element_type=F32)


def _dot_tn(a, b):
    return lax.dot_general(a, b, (((0,), (0,)), ((), ())), preferred_element_type=F32)


def _resident(shape):
    nd = len(shape)
    return pl.BlockSpec(shape, lambda *_: (0,) * nd, pipeline_mode=pl.Buffered(1))


def _layer_resident(layer, shape):
    nd = len(shape)
    return pl.BlockSpec((None,) + tuple(shape), lambda *_: (layer,) + (0,) * nd,
                        pipeline_mode=pl.Buffered(1))


def _mem_kv_body(mem_ref, g_ref, w_ref, k_ref, v_ref):
    m = _rms(mem_ref[...], g_ref[...]).astype(BF16)
    kv = _dot(m, w_ref[...])
    k_ref[...] = kv[:, :BRANCH_W].astype(BF16)
    v_ref[...] = kv[:, BRANCH_W:].astype(BF16)


def _mem_kv(mem, mem_norm_g, kv_w):
    B = mem.shape[0]
    out = jax.ShapeDtypeStruct((DEPTH, B, MEM_LEN, BRANCH_W), BF16)
    return pl.pallas_call(
        _mem_kv_body,
        grid=(DEPTH, B),
        in_specs=[
            pl.BlockSpec((None, MEM_LEN, D_MODEL), lambda l, b: (b, 0, 0)),
            pl.BlockSpec((None, 1, D_MODEL), lambda l, b: (l, 0, 0)),
            pl.BlockSpec((None, D_MODEL, 2 * BRANCH_W), lambda l, b: (l, 0, 0)),
        ],
        out_specs=[
            pl.BlockSpec((None, None, MEM_LEN, BRANCH_W), lambda l, b: (l, b, 0, 0)),
            pl.BlockSpec((None, None, MEM_LEN, BRANCH_W), lambda l, b: (l, b, 0, 0)),
        ],
        out_shape=[out, out],
        compiler_params=pltpu.CompilerParams(
            dimension_semantics=("arbitrary", "arbitrary"), vmem_limit_bytes=VMEM_LIMIT),
        name="mem_kv",
    )(mem, mem_norm_g.reshape(DEPTH, 1, D_MODEL), kv_w)


def _seq_proj_body(x_ref, ng_ref, wm_ref, wg_ref, wlow_ref, gkw_ref, gkb_ref,
                   h_ref, mq_ref, mk_ref, mv_ref, kmean_ref, gq_ref, gk_ref, gv_ref, gcum_ref,
                   glow_ref):
    T = x_ref.shape[0]
    half = T // 2
    q_scale = MOBA_HD ** -0.5 * LOG2_E
    h_top = _rms(x_ref[0:half, :], ng_ref[...]).astype(BF16)
    h_ref[0:half, :] = h_top
    mq_ref[0:half, :] = (_dot(h_top, wm_ref[:, 0:512]) * q_scale).astype(BF16)
    h_bot = _rms(x_ref[half:T, :], ng_ref[...]).astype(BF16)
    h_ref[half:T, :] = h_bot
    mq_ref[half:T, :] = (_dot(h_bot, wm_ref[:, 0:512]) * q_scale).astype(BF16)
    h = jnp.concatenate([h_top, h_bot], axis=0)
    low = _dot(h, wlow_ref[...]).astype(BF16)
    mk = _dot(h, wm_ref[:, 512:1024])
    mk_ref[...] = mk.astype(BF16)
    for j in range(T // MOBA_BLOCK):
        kmean_ref[j] = jnp.mean(mk[j * MOBA_BLOCK:(j + 1) * MOBA_BLOCK], axis=0, keepdims=True)
    z = _dot(low, gkw_ref[...]) + gkb_ref[...]
    mv_ref[...] = _dot(h, wm_ref[:, 1024:1536]).T.astype(BF16)
    log_sig = jnp.minimum(z, 0.0) - jnp.log(1.0 + jnp.exp(-jnp.abs(z)))
    g = log_sig / GLA_NORMALIZER
    gq_ref[...] = _dot(h, wg_ref[:, 0:256])
    gk_ref[...] = _dot(h, wg_ref[:, 256:512])

    C = GLA_CHUNK
    ri = lax.broadcasted_iota(jnp.int32, (C, C), 0)
    ci = lax.broadcasted_iota(jnp.int32, (C, C), 1)
    tri = jnp.where(ci <= ri, 1.0, 0.0).astype(BF16)
    lowest = None
    for c in range(T // C):
        gc = g[c * C:(c + 1) * C]
        hi = gc.astype(BF16)
        rest = gc - hi.astype(F32)
        mid = rest.astype(BF16)
        lo = (rest - mid.astype(F32)).astype(BF16)
        G = _dot(tri, hi) + _dot(tri, mid) + _dot(tri, lo)
        gcum_ref[c * C:(c + 1) * C, :] = G
        lowest = G[C - 1:C, :] if lowest is None else jnp.minimum(lowest, G[C - 1:C, :])
    glow_ref[...] = lowest
    gv_ref[...] = _dot(h, wg_ref[:, 512:1024]).astype(BF16)


def _seq_proj(layer, x, norm_g, w_m, w_g, w_low, gk_w, gk_b):
    B, S, _ = x.shape
    T = SEQ_TILE
    nblk = T // MOBA_BLOCK
    tok = lambda w: pl.BlockSpec((None, T, w), lambda b, i: (b, i, 0))
    sd = jax.ShapeDtypeStruct
    return pl.pallas_call(
        _seq_proj_body,
        grid=(B, S // T),
        in_specs=[
            tok(D_MODEL),
            _layer_resident(layer, (1, D_MODEL)),
            _layer_resident(layer, w_m.shape[1:]),
            _layer_resident(layer, w_g.shape[1:]),
            _layer_resident(layer, w_low.shape[1:]),
            _layer_resident(layer, gk_w.shape[1:]),
            _layer_resident(layer, (1, GLA_HEADS * GLA_DK)),
        ],
        out_specs=[
            tok(D_MODEL), tok(512), tok(512),
            pl.BlockSpec((None, 512, T), lambda b, i: (b, 0, i)),
            pl.BlockSpec((None, nblk, 1, 512), lambda b, i: (b, i, 0, 0)),
            tok(256), tok(256), tok(512), tok(256),
            pl.BlockSpec((None, None, 1, 256), lambda b, i: (b, i, 0, 0)),
        ],
        out_shape=[
            sd((B, S, D_MODEL), BF16),
            sd((B, S, 512), BF16), sd((B, S, 512), BF16), sd((B, 512, S), BF16),
            sd((B, S // MOBA_BLOCK, 1, 512), F32),
            sd((B, S, 256), F32), sd((B, S, 256), F32), sd((B, S, 512), BF16),
            sd((B, S, 256), F32), sd((B, S // T, 1, 256), F32),
        ],
        compiler_params=pltpu.CompilerParams(
            dimension_semantics=("arbitrary", "arbitrary"), vmem_limit_bytes=VMEM_LIMIT),
        name="seq_proj",
    )(x, norm_g.reshape(DEPTH, 1, D_MODEL), w_m, w_g, w_low, gk_w, gk_b.reshape(DEPTH, 1, -1))


def _moba_body(q_ref, k_ref, vt_ref, kmean_ref, o_ref, vt1_ref):
    BLK = MOBA_BLOCK
    hd = MOBA_HD
    S = q_ref.shape[0]
    nb = S // BLK

    def scores(i):
        return _dot_nt(k_ref[0:(i + 1) * BLK, :], q_ref[i * BLK:(i + 1) * BLK, :])

    st_next = scores(nb - 1)
    vt1_ref[0:hd, :] = vt_ref[...]
    vt1_ref[hd:, :] = jnp.ones((vt1_ref.shape[0] - hd, S), BF16)

    rs = _dot_nt(kmean_ref[...].astype(BF16), q_ref[...])
    blk = lax.broadcasted_iota(jnp.int32, (nb, S), 0)
    own = lax.broadcasted_iota(jnp.int32, (nb, S), 1) // BLK
    rs = jnp.where(blk < own, rs, NEG_INF)
    bias = jnp.full((nb, S), NEG_INF, F32)
    for _ in range(min(MOBA_TOPK, nb)):
        mx = jnp.max(rs, axis=0, keepdims=True)
        first = jnp.min(jnp.where(rs == mx, blk, nb), axis=0, keepdims=True)
        pick = (blk == first) & (mx > NEG_INF)
        bias = jnp.where(pick, 0.0, bias)
        rs = jnp.where(blk == first, NEG_INF, rs)

    key_i = lax.broadcasted_iota(jnp.int32, (BLK, BLK), 0)
    qry_i = lax.broadcasted_iota(jnp.int32, (BLK, BLK), 1)
    causal = key_i <= qry_i

    for i in reversed(range(nb)):
        qs = slice(i * BLK, (i + 1) * BLK)
        nk = (i + 1) * BLK
        st = st_next
        if i > 0:
            st_next = scores(i - 1)
        parts = [st[n * BLK:(n + 1) * BLK] + bias[n:n + 1, qs] for n in range(i)]
        parts.append(jnp.where(causal, st[i * BLK:nk], NEG_INF))
        mx = None
        for part in parts:
            pm = jnp.max(part.reshape(BLK // SUBLANES, SUBLANES, BLK), axis=0)
            mx = pm if mx is None else jnp.maximum(mx, pm)
        m = jnp.max(mx, axis=0, keepdims=True)
        pt = jnp.concatenate([jnp.exp2(part - m).astype(BF16) for part in parts], axis=0)
        ot = _dot(vt1_ref[:, 0:nk], pt)
        o_ref[qs, :] = (ot[0:hd] / ot[hd:hd + 1]).T.astype(o_ref.dtype)


def _moba(q, k, vt, kmean):
    B, S, _ = q.shape
    nb = S // MOBA_BLOCK
    seq = pl.BlockSpec((None, S, MOBA_HD), lambda b, h: (b, 0, h))
    return pl.pallas_call(
        _moba_body,
        grid=(B, MOBA_HEADS),
        in_specs=[
            seq, seq,
            pl.BlockSpec((None, MOBA_HD, S), lambda b, h: (b, h, 0)),
            pl.BlockSpec((None, nb, MOBA_HD), lambda b, h: (b, 0, h)),
        ],
        out_specs=seq,
        out_shape=jax.ShapeDtypeStruct((B, S, BRANCH_W), BF16),
        scratch_shapes=[pltpu.VMEM((MOBA_HD + BF16_ROWS, S), BF16)],
        compiler_params=pltpu.CompilerParams(
            dimension_semantics=("arbitrary", "arbitrary"), vmem_limit_bytes=VMEM_LIMIT),
        name="moba",
    )(q, k, vt, kmean)


def _gla_body(ok_ref, q_ref, k_ref, v_ref, gcum_ref, ng_ref, o_ref, state_ref, vf_ref):
    T = q_ref.shape[0]
    C = GLA_CHUNK
    H, dk, dv = GLA_HEADS, GLA_DK, GLA_DV
    n_chunks = T // C

    @pl.when(pl.program_id(1) == 0)
    def _():
        state_ref[...] = jnp.zeros_like(state_ref)

    ri = lax.broadcasted_iota(jnp.int32, (C, C), 0)
    ci = lax.broadcasted_iota(jnp.int32, (C, C), 1)
    causal = ci <= ri
    factorable = ok_ref[pl.program_id(0), pl.program_id(1) // (SEQ_TILE // GLA_TILE)] != 0

    def run_tile(intra_fn):
        ng = ng_ref[...]
        pending = []
        for c in range(n_chunks):
            rows = pl.ds(c * C, C)
            G = gcum_ref[rows, :]
            G_last = G[C - 1:C, :]
            q = q_ref[rows, :] * (dk ** -0.5)
            k = k_ref[rows, :]
            q_in = (q * jnp.exp(G)).astype(BF16)
            k_st = (k * jnp.exp(G_last - G)).astype(BF16)
            a_last = jnp.exp(G_last)
            v = v_ref[rows, :]
            intra = intra_fn(c, q, k, G, q_in, v)
            kv = [_dot_tn(v[:, h * dv:(h + 1) * dv], k_st[:, h * dk:(h + 1) * dk])
                  for h in range(H)]
            pending.append((q_in, a_last, intra, kv))
        states = [state_ref[h] for h in range(H)]
        entry = []
        for q_in, a_last, intra, kv in pending:
            entry.append(states)
            states = [a_last[:, h * dk:(h + 1) * dk] * states[h] + kv[h] for h in range(H)]
        for h in range(H):
            state_ref[h] = states[h]
        for c, (q_in, a_last, intra, kv) in enumerate(pending):
            rows = pl.ds(c * C, C)
            for h in range(H):
                ks = slice(h * dk, (h + 1) * dk)
                vs = slice(h * dv, (h + 1) * dv)
                o = intra[h] + _dot_nt(q_in[:, ks], entry[c][h].astype(BF16))
                o = o * lax.rsqrt(jnp.mean(o * o, axis=-1, keepdims=True) + EPS) * ng
                o_ref[rows, vs] = o.astype(o_ref.dtype)

    def intra_factored(c, q, k, G, q_in, v):
        k_out = (k * jnp.exp(-G)).astype(BF16)
        out = []
        for h in range(H):
            ks = slice(h * dk, (h + 1) * dk)
            A = jnp.where(causal, _dot_nt(q_in[:, ks], k_out[:, ks]), 0.0)
            out.append(_dot(A.astype(BF16), v[:, h * dv:(h + 1) * dv]))
        return out

    def intra_keywise(c, q, k, G, q_in, v):
        d_head = lax.broadcasted_iota(jnp.int32, (H * dk, H * dv), 0) // dk
        e_head = lax.broadcasted_iota(jnp.int32, (H * dk, H * dv), 1) // dv
        head_spread = jnp.where(d_head == e_head, 1.0, 0.0).astype(F32)
        row = lax.broadcasted_iota(jnp.int32, (C, H * dk), 0)

        def add_key(j, acc):
            r = c * C + j
            k_j = k_ref[pl.ds(r, 1), :]
            G_j = gcum_ref[pl.ds(r, 1), :]
            v_j = vf_ref[pl.ds(r, 1), :]
            w = jnp.where(row >= j, q * k_j * jnp.exp(jnp.minimum(G - G_j, 0.0)), 0.0)
            a = jnp.dot(w, head_spread, preferred_element_type=F32,
                        precision=lax.Precision.HIGHEST)
            return acc + a * v_j

        o = lax.fori_loop(0, C, add_key, jnp.zeros((C, H * dv), F32))
        return [o[:, h * dv:(h + 1) * dv] for h in range(H)]

    @pl.when(factorable)
    def _():
        run_tile(intra_factored)

    @pl.when(jnp.logical_not(factorable))
    def _():
        vf_ref[...] = v_ref[...].astype(F32)
        run_tile(intra_keywise)


def _gla(layer, q, k, v, gcum, glow, norm_g):
    B, S, _ = q.shape
    T = GLA_TILE
    ok = (jnp.min(glow, axis=(2, 3)) >= -GLA_SAFE_LOG_DECAY).astype(jnp.int32)
    tok = lambda w: pl.BlockSpec((None, T, w), lambda b, i, ok_ref: (b, i, 0))
    return pl.pallas_call(
        _gla_body,
        grid_spec=pltpu.PrefetchScalarGridSpec(
            num_scalar_prefetch=1,
            grid=(B, S // T),
            in_specs=[tok(256), tok(256), tok(512), tok(256),
                      _layer_resident(layer, (1, GLA_DV))],
            out_specs=tok(512),
            scratch_shapes=[pltpu.VMEM((GLA_HEADS, GLA_DV, GLA_DK), F32),
                            pltpu.VMEM((T, GLA_HEADS * GLA_DV), F32)]),
        out_shape=jax.ShapeDtypeStruct((B, S, BRANCH_W), BF16),
        compiler_params=pltpu.CompilerParams(
            dimension_semantics=("arbitrary", "arbitrary"), vmem_limit_bytes=VMEM_LIMIT),
        name="gla",
    )(ok, q, k, v, gcum, norm_g.reshape(-1, 1, GLA_DV))


def _merge_body(final, x_ref, h_ref, ymoba_ref, ygla_ref, memk_ref, memv_ref, wc_ref, wmz_ref,
                wr_ref, convw_ref, convb_ref, lng_ref, lnb_ref, wb_ref, wo_ref, fg_ref,
                o_ref, u_ref):
    T = x_ref.shape[0]
    W = BRANCH_W

    @pl.when(pl.program_id(1) == 0)
    def _():
        u_ref[0, 0:CONV_HALO, :] = jnp.zeros((CONV_HALO, W), F32)

    h = h_ref[...]

    def proj(w):
        return _dot(h, w)

    u_ref[0, CONV_HALO:CONV_HALO + T, :] = (
        proj(wc_ref[:, 0:W]) * _sigmoid(proj(wc_ref[:, W:2 * W])))
    span = T + CONV_HALO - SUBLANES
    for s in range(1, SUBLANES):
        u_ref[s, 0:span, :] = u_ref[0, s:s + span, :]
    R = 32
    base = CONV_HALO - (CONV_K - 1)
    conv_rows = []
    chunks = T // R
    gate_cols = [wr_ref.at[:, 3 * W + n * D_MODEL:3 * W + (n + 1) * D_MODEL]
                 for n in range(N_BRANCH)]
    early_cols = gate_cols + [wmz_ref, wr_ref.at[:, 0:W], wr_ref.at[:, W:2 * W],
                              wr_ref.at[:, 2 * W:3 * W]]
    assert len(early_cols) <= chunks
    early = []
    for r in range(chunks):
        if r < len(early_cols):
            early.append(proj(early_cols[r][...]))
        acc = jnp.zeros((R, W), F32) + convb_ref[...]
        for j in range(CONV_K):
            a, s = divmod(base + j, SUBLANES)
            lo = a * SUBLANES + r * R
            acc = acc + convw_ref[j:j + 1, :] * u_ref[s, lo:lo + R, :]
        conv_rows.append(acc)
    yc = jnp.concatenate(conv_rows, axis=0)
    u_ref[0, 0:CONV_HALO, :] = u_ref[0, T:T + CONV_HALO, :]
    gate_logits = early[:N_BRANCH]
    moba_z, gla_z, mem_q, mem_z = early[N_BRANCH:]
    y_moba = (ymoba_ref[...].astype(F32) * _silu(moba_z)).astype(BF16)
    y_gla = (ygla_ref[...].astype(F32) * _silu(gla_z)).astype(BF16)

    xq = (mem_q * (MEM_HD ** -0.5)).astype(BF16)
    heads = []
    for hd in range(MEM_HEADS):
        ls = slice(hd * MEM_HD, (hd + 1) * MEM_HD)
        s = _dot_nt(xq[:, ls], memk_ref[:, ls])
        p = jnp.exp(s - jnp.max(s, axis=-1, keepdims=True))
        l = jnp.sum(p, axis=-1, keepdims=True)
        heads.append(_dot(p.astype(BF16), memv_ref[:, ls]) / l)
    y_mem = (jnp.concatenate(heads, axis=-1) * _silu(mem_z)).astype(BF16)

    mu = jnp.mean(yc, axis=-1, keepdims=True)
    yc = yc - mu
    var = jnp.mean(yc * yc, axis=-1, keepdims=True)
    yc = _silu(yc * lax.rsqrt(var + EPS) * lng_ref[...] + lnb_ref[...])
    y_conv = (yc * _silu(proj(wc_ref[:, 2 * W:3 * W]))).astype(BF16)

    merged = jnp.zeros((T, D_MODEL), F32)
    for n, y in ((1, y_moba), (2, y_gla), (3, y_mem), (0, y_conv)):
        merged = merged + _sigmoid(gate_logits[n]) * _dot(y, wb_ref[n])
    out = x_ref[...] + _dot(merged.astype(BF16), wo_ref[...])
    if final:
        out = _rms(out, fg_ref[...])
    o_ref[...] = out


def _merge(layer, x, h, y_moba, y_gla, mem_k, mem_v, w_c, w_mz, w_r, conv_w, conv_b, ln_g, ln_b,
           w_branch, w_out, final_g):
    B, S, _ = x.shape
    T = MERGE_TILE
    final = layer == DEPTH - 1
    tok = lambda w: pl.BlockSpec((None, T, w), lambda b, i: (b, i, 0))
    mem = pl.BlockSpec((None, None, MEM_LEN, BRANCH_W), lambda b, i: (layer, b, 0, 0))
    row = lambda a: a.reshape(DEPTH, 1, -1)
    return pl.pallas_call(
        functools.partial(_merge_body, final),
        grid=(B, S // T),
        in_specs=[
            tok(D_MODEL), tok(D_MODEL), tok(BRANCH_W), tok(BRANCH_W), mem, mem,
            _layer_resident(layer, w_c.shape[1:]),
            _layer_resident(layer, w_mz.shape[1:]),
            _layer_resident(layer, w_r.shape[1:]),
            _layer_resident(layer, conv_w.shape[1:]),
            _layer_resident(layer, (1, BRANCH_W)), _layer_resident(layer, (1, BRANCH_W)),
            _layer_resident(layer, (1, BRANCH_W)),
            _layer_resident(layer, w_branch.shape[1:]),
            _layer_resident(layer, w_out.shape[1:]),
            _resident((1, D_MODEL)),
        ],
        out_specs=tok(D_MODEL),
        out_shape=jax.ShapeDtypeStruct((B, S, D_MODEL), F32),
        scratch_shapes=[pltpu.VMEM((SUBLANES, T + CONV_HALO, BRANCH_W), F32)],
        compiler_params=pltpu.CompilerParams(
            dimension_semantics=("arbitrary", "arbitrary"), vmem_limit_bytes=VMEM_LIMIT),
        name="merge_final" if final else "merge",
    )(x, h, y_moba, y_gla, mem_k, mem_v, w_c, w_mz, w_r, conv_w, row(conv_b), row(ln_g),
      row(ln_b), w_branch, w_out, final_g.reshape(1, -1))


def kernel(x, mem, norm_g, w_in, conv_w, conv_b, conv_ln_g, conv_ln_b, gla_gk_w, gla_gk_b,
           gla_norm_g, mem_norm_g, mem_kv_w, w_branch, w_out, final_norm_g):
    w_m = w_in[:, :, _M_Q:_M_Z].astype(BF16)
    w_g = w_in[:, :, _G_Q:_G_LOW].astype(BF16)
    w_low = jnp.pad(w_in[:, :, _G_LOW:_G_Z], ((0, 0), (0, 0), (0, LANES - GLA_RANK))).astype(BF16)
    gk_w = jnp.pad(gla_gk_w, ((0, 0), (0, LANES - GLA_RANK), (0, 0))).astype(BF16)
    w_c = w_in[:, :, _C_VAL:_M_Q].astype(BF16)
    w_mz = w_in[:, :, _M_Z:_G_Q].astype(BF16)
    w_r = w_in[:, :, _G_Z:_END].astype(BF16)
    w_branch = w_branch.astype(BF16)
    w_out = w_out.astype(BF16)

    mem_k, mem_v = _mem_kv(mem, mem_norm_g, mem_kv_w.astype(BF16))
    for l in range(DEPTH):
        h, mq, mk, mv, kmean, gq, gk, gv, gcum, glow = _seq_proj(
            l, x, norm_g, w_m, w_g, w_low, gk_w, gla_gk_b)
        y_moba = _moba(mq, mk, mv, kmean.reshape(x.shape[0], -1, BRANCH_W))
        y_gla = _gla(l, gq, gk, gv, gcum, glow, gla_norm_g)
        x = _merge(l, x, h, y_moba, y_gla, mem_k, mem_v, w_c, w_mz, w_r, conv_w, conv_b,
                   conv_ln_g, conv_ln_b, w_branch, w_out, final_norm_g)
    return x
```

```python
import functools

import jax
import jax.numpy as jnp
import numpy as np
from jax import lax
from jax.experimental import pallas as pl
from jax.experimental.pallas import tpu as pltpu

D_MODEL = 1024
DEPTH = 2
MEM_LEN = 256
N_BRANCH = 4
BRANCH_W = 512
CONV_K = 31
MOBA_HEADS = 4
MOBA_HD = 128
MOBA_BLOCK = 256
MOBA_TOPK = 3
GLA_HEADS = 4
GLA_DK = 64
GLA_DV = 128
GLA_RANK = 16
GLA_NORMALIZER = 16.0
MEM_HEADS = 4
MEM_HD = 128
EPS = 1e-6

LANES = 128
SUBLANES = 8
BF16_ROWS = 2 * SUBLANES
CONV_HALO = 32
GLA_CHUNK = 128
GLA_SAFE_LOG_DECAY = 60.0
SEQ_TILE = 1024
GLA_TILE = 512
MERGE_TILE = 256
VMEM_LIMIT = 56 * 1024 * 1024

F32 = jnp.float32
BF16 = jnp.bfloat16
NEG_INF = float("-inf")
LOG2_E = 1.4426950408889634

_SPLITS = (512, 512, 512, 512, 512, 512, 512, 256, 256, 512, GLA_RANK, 512, 512, 512, 4096)
_OFF = tuple(int(o) for o in np.cumsum((0,) + _SPLITS))
(_C_VAL, _C_GLU, _C_Z, _M_Q, _M_K, _M_V, _M_Z, _G_Q, _G_K, _G_V, _G_LOW, _G_Z, _X_Q, _X_Z,
 _GATES, _END) = _OFF


def _rms(xf, g):
    return xf * lax.rsqrt(jnp.mean(xf * xf, axis=-1, keepdims=True) + EPS) * g


def _sigmoid(x):
    return 1.0 / (1.0 + jnp.exp(-x))


def _silu(x):
    return x * _sigmoid(x)


def _dot(a, b):
    return jnp.dot(a, b, preferred_element_type=F32)


def _dot_nt(a, b):
    return lax.dot_general(a, b, (((1,), (1,)), ((), ())), preferred_element_type=F32)


def _dot_tn(a, b):
    return lax.dot_general(a, b, (((0,), (0,)), ((), ())), preferred_element_type=F32)


def _resident(shape):
    nd = len(shape)
    return pl.BlockSpec(shape, lambda *_: (0,) * nd, pipeline_mode=pl.Buffered(1))


def _layer_resident(layer, shape):
    nd = len(shape)
    return pl.BlockSpec((None,) + tuple(shape), lambda *_: (layer,) + (0,) * nd,
                        pipeline_mode=pl.Buffered(1))


def _mem_kv_body(mem_ref, g_ref, w_ref, k_ref, v_ref):
    m = _rms(mem_ref[...], g_ref[...]).astype(BF16)
    kv = _dot(m, w_ref[...])
    k_ref[...] = kv[:, :BRANCH_W].astype(BF16)
    v_ref[...] = kv[:, BRANCH_W:].astype(BF16)


def _mem_kv(mem, mem_norm_g, kv_w):
    B = mem.shape[0]
    out = jax.ShapeDtypeStruct((DEPTH, B, MEM_LEN, BRANCH_W), BF16)
    return pl.pallas_call(
        _mem_kv_body,
        grid=(DEPTH, B),
        in_specs=[
            pl.BlockSpec((None, MEM_LEN, D_MODEL), lambda l, b: (b, 0, 0)),
            pl.BlockSpec((None, 1, D_MODEL), lambda l, b: (l, 0, 0)),
            pl.BlockSpec((None, D_MODEL, 2 * BRANCH_W), lambda l, b: (l, 0, 0)),
        ],
        out_specs=[
            pl.BlockSpec((None, None, MEM_LEN, BRANCH_W), lambda l, b: (l, b, 0, 0)),
            pl.BlockSpec((None, None, MEM_LEN, BRANCH_W), lambda l, b: (l, b, 0, 0)),
        ],
        out_shape=[out, out],
        compiler_params=pltpu.CompilerParams(
            dimension_semantics=("arbitrary", "arbitrary"), vmem_limit_bytes=VMEM_LIMIT),
        name="mem_kv",
    )(mem, mem_norm_g.reshape(DEPTH, 1, D_MODEL), kv_w)


def _decay_w_body(wlow_ref, gkw_ref, o_ref):
    o_ref[...] = jnp.dot(wlow_ref[...], gkw_ref[...], preferred_element_type=F32,
                         precision=lax.Precision.HIGHEST).astype(BF16)


def _decay_w(w_low, gk_w):
    return pl.pallas_call(
        _decay_w_body,
        grid=(DEPTH,),
        in_specs=[pl.BlockSpec((None,) + w_low.shape[1:], lambda l: (l, 0, 0)),
                  pl.BlockSpec((None,) + gk_w.shape[1:], lambda l: (l, 0, 0))],
        out_specs=pl.BlockSpec((None, D_MODEL, GLA_HEADS * GLA_DK), lambda l: (l, 0, 0)),
        out_shape=jax.ShapeDtypeStruct((DEPTH, D_MODEL, GLA_HEADS * GLA_DK), BF16),
        name="decay_w",
    )(w_low, gk_w)


def _seq_proj_body(x_ref, ng_ref, wm_ref, wg_ref, wdec_ref, gkb_ref,
                   h_ref, mq_ref, mk_ref, mv_ref, kmean_ref, gq_ref, gk_ref, gv_ref, gcum_ref,
                   glow_ref):
    T = x_ref.shape[0]
    half = T // 2
    q_scale = MOBA_HD ** -0.5 * LOG2_E
    h_top = _rms(x_ref[0:half, :], ng_ref[...]).astype(BF16)
    h_ref[0:half, :] = h_top
    mq_ref[0:half, :] = (_dot(h_top, wm_ref[:, 0:512]) * q_scale).astype(BF16)
    h_bot = _rms(x_ref[half:T, :], ng_ref[...]).astype(BF16)
    h_ref[half:T, :] = h_bot
    mq_ref[half:T, :] = (_dot(h_bot, wm_ref[:, 0:512]) * q_scale).astype(BF16)
    h = jnp.concatenate([h_top, h_bot], axis=0)
    z = _dot(h, wdec_ref[...]) + gkb_ref[...]
    mk = _dot(h, wm_ref[:, 512:1024])
    mk_ref[...] = mk.astype(BF16)
    for j in range(T // MOBA_BLOCK):
        kmean_ref[j] = jnp.mean(mk[j * MOBA_BLOCK:(j + 1) * MOBA_BLOCK], axis=0, keepdims=True)
    mv_ref[...] = _dot(h, wm_ref[:, 1024:1536]).T.astype(BF16)
    log_sig = jnp.minimum(z, 0.0) - jnp.log(1.0 + jnp.exp(-jnp.abs(z)))
    g = log_sig / GLA_NORMALIZER
    gq_ref[...] = _dot(h, wg_ref[:, 0:256])
    gk_ref[...] = _dot(h, wg_ref[:, 256:512])

    C = GLA_CHUNK
    ri = lax.broadcasted_iota(jnp.int32, (C, C), 0)
    ci = lax.broadcasted_iota(jnp.int32, (C, C), 1)
    tri = jnp.where(ci <= ri, 1.0, 0.0).astype(BF16)
    lowest = None
    for c in range(T // C):
        gc = g[c * C:(c + 1) * C]
        hi = gc.astype(BF16)
        rest = gc - hi.astype(F32)
        mid = rest.astype(BF16)
        lo = (rest - mid.astype(F32)).astype(BF16)
        G = _dot(tri, hi) + _dot(tri, mid) + _dot(tri, lo)
        gcum_ref[c * C:(c + 1) * C, :] = G
        lowest = G[C - 1:C, :] if lowest is None else jnp.minimum(lowest, G[C - 1:C, :])
    glow_ref[...] = lowest
    gv_ref[...] = _dot(h, wg_ref[:, 512:1024]).astype(BF16)


def _seq_proj(layer, x, norm_g, w_m, w_g, w_dec, gk_b):
    B, S, _ = x.shape
    T = SEQ_TILE
    nblk = T // MOBA_BLOCK
    tok = lambda w: pl.BlockSpec((None, T, w), lambda b, i: (b, i, 0))
    sd = jax.ShapeDtypeStruct
    return pl.pallas_call(
        _seq_proj_body,
        grid=(B, S // T),
        in_specs=[
            tok(D_MODEL),
            _layer_resident(layer, (1, D_MODEL)),
            _layer_resident(layer, w_m.shape[1:]),
            _layer_resident(layer, w_g.shape[1:]),
            _layer_resident(layer, w_dec.shape[1:]),
            _layer_resident(layer, (1, GLA_HEADS * GLA_DK)),
        ],
        out_specs=[
            tok(D_MODEL), tok(512), tok(512),
            pl.BlockSpec((None, 512, T), lambda b, i: (b, 0, i)),
            pl.BlockSpec((None, nblk, 1, 512), lambda b, i: (b, i, 0, 0)),
            tok(256), tok(256), tok(512), tok(256),
            pl.BlockSpec((None, None, 1, 256), lambda b, i: (b, i, 0, 0)),
        ],
        out_shape=[
            sd((B, S, D_MODEL), BF16),
            sd((B, S, 512), BF16), sd((B, S, 512), BF16), sd((B, 512, S), BF16),
            sd((B, S // MOBA_BLOCK, 1, 512), F32),
            sd((B, S, 256), F32), sd((B, S, 256), F32), sd((B, S, 512), BF16),
            sd((B, S, 256), F32), sd((B, S // T, 1, 256), F32),
        ],
        compiler_params=pltpu.CompilerParams(
            dimension_semantics=("arbitrary", "arbitrary"), vmem_limit_bytes=VMEM_LIMIT),
        name="seq_proj",
    )(x, norm_g.reshape(DEPTH, 1, D_MODEL), w_m, w_g, w_dec, gk_b.reshape(DEPTH, 1, -1))


def _moba_body(q_ref, k_ref, vt_ref, kmean_ref, o_ref, vt1_ref):
    BLK = MOBA_BLOCK
    hd = MOBA_HD
    S = q_ref.shape[0]
    nb = S // BLK

    def scores(i):
        return _dot_nt(k_ref[0:(i + 1) * BLK, :], q_ref[i * BLK:(i + 1) * BLK, :])

    st_next = scores(nb - 1)
    vt1_ref[0:hd, :] = vt_ref[...]
    vt1_ref[hd:, :] = jnp.ones((vt1_ref.shape[0] - hd, S), BF16)

    rs = _dot_nt(kmean_ref[...].astype(BF16), q_ref[...])
    blk = lax.broadcasted_iota(jnp.int32, (nb, S), 0)
    own = lax.broadcasted_iota(jnp.int32, (nb, S), 1) // BLK
    rs = jnp.where(blk < own, rs, NEG_INF)
    bias = jnp.full((nb, S), NEG_INF, F32)
    for _ in range(min(MOBA_TOPK, nb)):
        mx = jnp.max(rs, axis=0, keepdims=True)
        first = jnp.min(jnp.where(rs == mx, blk, nb), axis=0, keepdims=True)
        pick = (blk == first) & (mx > NEG_INF)
        bias = jnp.where(pick, 0.0, bias)
        rs = jnp.where(blk == first, NEG_INF, rs)

    key_i = lax.broadcasted_iota(jnp.int32, (BLK, BLK), 0)
    qry_i = lax.broadcasted_iota(jnp.int32, (BLK, BLK), 1)
    causal = key_i <= qry_i

    for i in reversed(range(nb)):
        qs = slice(i * BLK, (i + 1) * BLK)
        nk = (i + 1) * BLK
        st = st_next
        if i > 0:
            st_next = scores(i - 1)
        parts = [st[n * BLK:(n + 1) * BLK] + bias[n:n + 1, qs] for n in range(i)]
        parts.append(jnp.where(causal, st[i * BLK:nk], NEG_INF))
        mx = None
        for part in parts:
            pm = jnp.max(part.reshape(BLK // SUBLANES, SUBLANES, BLK), axis=0)
            mx = pm if mx is None else jnp.maximum(mx, pm)
        m = jnp.max(mx, axis=0, keepdims=True)
        pt = jnp.concatenate([jnp.exp2(part - m).astype(BF16) for part in parts], axis=0)
        ot = _dot(vt1_ref[:, 0:nk], pt)
        o_ref[qs, :] = (ot[0:hd] / ot[hd:hd + 1]).T.astype(o_ref.dtype)


def _moba(q, k, vt, kmean):
    B, S, _ = q.shape
    nb = S // MOBA_BLOCK
    seq = pl.BlockSpec((None, S, MOBA_HD), lambda b, h: (b, 0, h))
    return pl.pallas_call(
        _moba_body,
        grid=(B, MOBA_HEADS),
        in_specs=[
            seq, seq,
            pl.BlockSpec((None, MOBA_HD, S), lambda b, h: (b, h, 0)),
            pl.BlockSpec((None, nb, MOBA_HD), lambda b, h: (b, 0, h)),
        ],
        out_specs=seq,
        out_shape=jax.ShapeDtypeStruct((B, S, BRANCH_W), BF16),
        scratch_shapes=[pltpu.VMEM((MOBA_HD + BF16_ROWS, S), BF16)],
        compiler_params=pltpu.CompilerParams(
            dimension_semantics=("arbitrary", "arbitrary"), vmem_limit_bytes=VMEM_LIMIT),
        name="moba",
    )(q, k, vt, kmean)


def _gla_body(q_ref, k_ref, v_ref, gcum_ref, glow_ref, ng_ref, o_ref, state_ref, vf_ref):
    T = q_ref.shape[0]
    C = GLA_CHUNK
    H, dk, dv = GLA_HEADS, GLA_DK, GLA_DV
    n_chunks = T // C

    @pl.when(pl.program_id(1) == 0)
    def _():
        state_ref[...] = jnp.zeros_like(state_ref)

    ri = lax.broadcasted_iota(jnp.int32, (C, C), 0)
    ci = lax.broadcasted_iota(jnp.int32, (C, C), 1)
    causal = ci <= ri
    factorable = jnp.min(glow_ref[...]) >= -GLA_SAFE_LOG_DECAY

    def run_tile(intra_fn):
        ng = ng_ref[...]
        pending = []
        for c in range(n_chunks):
            rows = pl.ds(c * C, C)
            G = gcum_ref[rows, :]
            G_last = G[C - 1:C, :]
            q = q_ref[rows, :] * (dk ** -0.5)
            k = k_ref[rows, :]
            q_in = (q * jnp.exp(G)).astype(BF16)
            k_st = (k * jnp.exp(G_last - G)).astype(BF16)
            a_last = jnp.exp(G_last)
            v = v_ref[rows, :]
            intra = intra_fn(c, q, k, G, q_in, v)
            kv = [_dot_tn(v[:, h * dv:(h + 1) * dv], k_st[:, h * dk:(h + 1) * dk])
                  for h in range(H)]
            pending.append((q_in, a_last, intra, kv))
        states = [state_ref[h] for h in range(H)]
        entry = []
        for q_in, a_last, intra, kv in pending:
            entry.append(states)
            states = [a_last[:, h * dk:(h + 1) * dk] * states[h] + kv[h] for h in range(H)]
        for h in range(H):
            state_ref[h] = states[h]
        for c, (q_in, a_last, intra, kv) in enumerate(pending):
            rows = pl.ds(c * C, C)
            for h in range(H):
                ks = slice(h * dk, (h + 1) * dk)
                vs = slice(h * dv, (h + 1) * dv)
                o = intra[h] + _dot_nt(q_in[:, ks], entry[c][h].astype(BF16))
                o = o * lax.rsqrt(jnp.mean(o * o, axis=-1, keepdims=True) + EPS) * ng
                o_ref[rows, vs] = o.astype(o_ref.dtype)

    def intra_factored(c, q, k, G, q_in, v):
        k_out = (k * jnp.exp(-G)).astype(BF16)
        out = []
        for h in range(H):
            ks = slice(h * dk, (h + 1) * dk)
            A = jnp.where(causal, _dot_nt(q_in[:, ks], k_out[:, ks]), 0.0)
            out.append(_dot(A.astype(BF16), v[:, h * dv:(h + 1) * dv]))
        return out

    def intra_keywise(c, q, k, G, q_in, v):
        d_head = lax.broadcasted_iota(jnp.int32, (H * dk, H * dv), 0) // dk
        e_head = lax.broadcasted_iota(jnp.int32, (H * dk, H * dv), 1) // dv
        head_spread = jnp.where(d_head == e_head, 1.0, 0.0).astype(F32)
        row = lax.broadcasted_iota(jnp.int32, (C, H * dk), 0)

        def add_key(j, acc):
            r = c * C + j
            k_j = k_ref[pl.ds(r, 1), :]
            G_j = gcum_ref[pl.ds(r, 1), :]
            v_j = vf_ref[pl.ds(r, 1), :]
            w = jnp.where(row >= j, q * k_j * jnp.exp(jnp.minimum(G - G_j, 0.0)), 0.0)
            a = jnp.dot(w, head_spread, preferred_element_type=F32,
                        precision=lax.Precision.HIGHEST)
            return acc + a * v_j

        o = lax.fori_loop(0, C, add_key, jnp.zeros((C, H * dv), F32))
        return [o[:, h * dv:(h + 1) * dv] for h in range(H)]

    @pl.when(factorable)
    def _():
        run_tile(intra_factored)

    @pl.when(jnp.logical_not(factorable))
    def _():
        vf_ref[...] = v_ref[...].astype(F32)
        run_tile(intra_keywise)


def _gla(layer, q, k, v, gcum, glow, norm_g):
    B, S, _ = q.shape
    T = GLA_TILE
    per_guard = SEQ_TILE // GLA_TILE
    tok = lambda w: pl.BlockSpec((None, T, w), lambda b, i: (b, i, 0))
    return pl.pallas_call(
        _gla_body,
        grid=(B, S // T),
        in_specs=[tok(256), tok(256), tok(512), tok(256),
                  pl.BlockSpec((None, None, 1, 256), lambda b, i: (b, i // per_guard, 0, 0)),
                  _layer_resident(layer, (1, GLA_DV))],
        out_specs=tok(512),
        out_shape=jax.ShapeDtypeStruct((B, S, BRANCH_W), BF16),
        scratch_shapes=[pltpu.VMEM((GLA_HEADS, GLA_DV, GLA_DK), F32),
                        pltpu.VMEM((T, GLA_HEADS * GLA_DV), F32)],
        compiler_params=pltpu.CompilerParams(
            dimension_semantics=("arbitrary", "arbitrary"), vmem_limit_bytes=VMEM_LIMIT),
        name="gla",
    )(q, k, v, gcum, glow, norm_g.reshape(-1, 1, GLA_DV))


def _merge_body(final, x_ref, h_ref, ymoba_ref, ygla_ref, memk_ref, memv_ref, wc_ref, wmz_ref,
                wr_ref, convw_ref, convb_ref, lng_ref, lnb_ref, wb_ref, wo_ref, fg_ref,
                o_ref, u_ref):
    T = x_ref.shape[0]
    W = BRANCH_W

    @pl.when(pl.program_id(1) == 0)
    def _():
        u_ref[0, 0:CONV_HALO, :] = jnp.zeros((CONV_HALO, W), F32)

    h = h_ref[...]

    def proj(w):
        return _dot(h, w)

    u_ref[0, CONV_HALO:CONV_HALO + T, :] = (
        proj(wc_ref[:, 0:W]) * _sigmoid(proj(wc_ref[:, W:2 * W])))
    span = T + CONV_HALO - SUBLANES
    for s in range(1, SUBLANES):
        u_ref[s, 0:span, :] = u_ref[0, s:s + span, :]
    R = 32
    base = CONV_HALO - (CONV_K - 1)
    conv_rows = []
    chunks = T // R
    gate_cols = [wr_ref.at[:, 3 * W + n * D_MODEL:3 * W + (n + 1) * D_MODEL]
                 for n in range(N_BRANCH)]
    early_cols = gate_cols + [wmz_ref, wr_ref.at[:, 0:W], wr_ref.at[:, W:2 * W],
                              wr_ref.at[:, 2 * W:3 * W]]
    assert len(early_cols) <= chunks
    early = []
    for r in range(chunks):
        if r < len(early_cols):
            early.append(proj(early_cols[r][...]))
        acc = jnp.zeros((R, W), F32) + convb_ref[...]
        for j in range(CONV_K):
            a, s = divmod(base + j, SUBLANES)
            lo = a * SUBLANES + r * R
            acc = acc + convw_ref[j:j + 1, :] * u_ref[s, lo:lo + R, :]
        conv_rows.append(acc)
    yc = jnp.concatenate(conv_rows, axis=0)
    u_ref[0, 0:CONV_HALO, :] = u_ref[0, T:T + CONV_HALO, :]
    gate_logits = early[:N_BRANCH]
    moba_z, gla_z, mem_q, mem_z = early[N_BRANCH:]
    y_moba = (ymoba_ref[...].astype(F32) * _silu(moba_z)).astype(BF16)
    y_gla = (ygla_ref[...].astype(F32) * _silu(gla_z)).astype(BF16)

    xq = (mem_q * (MEM_HD ** -0.5)).astype(BF16)
    heads = []
    for hd in range(MEM_HEADS):
        ls = slice(hd * MEM_HD, (hd + 1) * MEM_HD)
        s = _dot_nt(xq[:, ls], memk_ref[:, ls])
        p = jnp.exp(s - jnp.max(s, axis=-1, keepdims=True))
        l = jnp.sum(p, axis=-1, keepdims=True)
        heads.append(_dot(p.astype(BF16), memv_ref[:, ls]) / l)
    y_mem = (jnp.concatenate(heads, axis=-1) * _silu(mem_z)).astype(BF16)

    mu = jnp.mean(yc, axis=-1, keepdims=True)
    yc = yc - mu
    var = jnp.mean(yc * yc, axis=-1, keepdims=True)
    yc = _silu(yc * lax.rsqrt(var + EPS) * lng_ref[...] + lnb_ref[...])
    y_conv = (yc * _silu(proj(wc_ref[:, 2 * W:3 * W]))).astype(BF16)

    merged = jnp.zeros((T, D_MODEL), F32)
    for n, y in ((1, y_moba), (2, y_gla), (3, y_mem), (0, y_conv)):
        merged = merged + _sigmoid(gate_logits[n]) * _dot(y, wb_ref[n])
    out = x_ref[...] + _dot(merged.astype(BF16), wo_ref[...])
    if final:
        out = _rms(out, fg_ref[...])
    o_ref[...] = out


def _merge(layer, x, h, y_moba, y_gla, mem_k, mem_v, w_c, w_mz, w_r, conv_w, conv_b, ln_g, ln_b,
           w_branch, w_out, final_g):
    B, S, _ = x.shape
    T = MERGE_TILE
    final = layer == DEPTH - 1
    tok = lambda w: pl.BlockSpec((None, T, w), lambda b, i: (b, i, 0))
    mem = pl.BlockSpec((None, None, MEM_LEN, BRANCH_W), lambda b, i: (layer, b, 0, 0))
    row = lambda a: a.reshape(DEPTH, 1, -1)
    return pl.pallas_call(
        functools.partial(_merge_body, final),
        grid=(B, S // T),
        in_specs=[
            tok(D_MODEL), tok(D_MODEL), tok(BRANCH_W), tok(BRANCH_W), mem, mem,
            _layer_resident(layer, w_c.shape[1:]),
            _layer_resident(layer, w_mz.shape[1:]),
            _layer_resident(layer, w_r.shape[1:]),
            _layer_resident(layer, conv_w.shape[1:]),
            _layer_resident(layer, (1, BRANCH_W)), _layer_resident(layer, (1, BRANCH_W)),
            _layer_resident(layer, (1, BRANCH_W)),
            _layer_resident(layer, w_branch.shape[1:]),
            _layer_resident(layer, w_out.shape[1:]),
            _resident((1, D_MODEL)),
        ],
        out_specs=tok(D_MODEL),
        out_shape=jax.ShapeDtypeStruct((B, S, D_MODEL), F32),
        scratch_shapes=[pltpu.VMEM((SUBLANES, T + CONV_HALO, BRANCH_W), F32)],
        compiler_params=pltpu.CompilerParams(
            dimension_semantics=("arbitrary", "arbitrary"), vmem_limit_bytes=VMEM_LIMIT),
        name="merge_final" if final else "merge",
    )(x, h, y_moba, y_gla, mem_k, mem_v, w_c, w_mz, w_r, conv_w, row(conv_b), row(ln_g),
      row(ln_b), w_branch, w_out, final_g.reshape(1, -1))


def kernel(x, mem, norm_g, w_in, conv_w, conv_b, conv_ln_g, conv_ln_b, gla_gk_w, gla_gk_b,
           gla_norm_g, mem_norm_g, mem_kv_w, w_branch, w_out, final_norm_g):
    w_m = w_in[:, :, _M_Q:_M_Z].astype(BF16)
    w_g = w_in[:, :, _G_Q:_G_LOW].astype(BF16)
    w_low = jnp.pad(w_in[:, :, _G_LOW:_G_Z], ((0, 0), (0, 0), (0, LANES - GLA_RANK)))
    gk_w = jnp.pad(gla_gk_w, ((0, 0), (0, LANES - GLA_RANK), (0, 0)))
    w_dec = _decay_w(w_low, gk_w)
    w_c = w_in[:, :, _C_VAL:_M_Q].astype(BF16)
    w_mz = w_in[:, :, _M_Z:_G_Q].astype(BF16)
    w_r = w_in[:, :, _G_Z:_END].astype(BF16)
    w_branch = w_branch.astype(BF16)
    w_out = w_out.astype(BF16)

    mem_k, mem_v = _mem_kv(mem, mem_norm_g, mem_kv_w.astype(BF16))
    for l in range(DEPTH):
        h, mq, mk, mv, kmean, gq, gk, gv, gcum, glow = _seq_proj(
            l, x, norm_g, w_m, w_g, w_dec, gla_gk_b)
        y_moba = _moba(mq, mk, mv, kmean.reshape(x.shape[0], -1, BRANCH_W))
        y_gla = _gla(l, gq, gk, gv, gcum, glow, gla_norm_g)
        x = _merge(l, x, h, y_moba, y_gla, mem_k, mem_v, w_c, w_mz, w_r, conv_w, conv_b,
                   conv_ln_g, conv_ln_b, w_branch, w_out, final_norm_g)
    return x
```
